```python
import jax, jax.numpy as jnp
from jax import lax
import numpy as np

D_MODEL = 1024
BATCH = 8
SEQ = 2048
DEPTH = 1

PLE_DIM = 256
HEAD_DIM = 64
RWKV_HEADS = 8
RWKV_DIM = RWKV_HEADS * HEAD_DIM
DECAY_LORA = 64
ICLR_LORA = 64
GATE_LORA = 128
GN_EPS = 64e-5
ATTN_GROUPS = ((128, 1), (512, 4), (2048, 16))
HEADS_PER_GROUP = 4
ATTN_HEADS = HEADS_PER_GROUP * len(ATTN_GROUPS)
ATTN_DIM = ATTN_HEADS * HEAD_DIM
ATTN_OUT_DIM = HEADS_PER_GROUP * HEAD_DIM
BAND_BLOCK = 128
ROPE_THETA = 10000.0
NEG_INF = -1e30
D_FF = 2816
RMS_EPS = 1e-6
N_BRANCHES = 2
RWKV_COLS = 3 * RWKV_DIM + DECAY_LORA + ICLR_LORA + GATE_LORA
ATTN_COLS = 3 * ATTN_DIM
GATE_COLS = N_BRANCHES * D_MODEL
IN_COLS = RWKV_COLS + ATTN_COLS + GATE_COLS

kernel_name = 'hybrid_rwkv7_dilated_attn_macaron_block'


def rms_norm(x, gain):
    xf = x.astype(jnp.float32)
    y = xf * lax.rsqrt(jnp.mean(xf * xf, axis=-1, keepdims=True) + RMS_EPS)
    return (y * gain.astype(jnp.float32)).astype(x.dtype)


def swiglu(h, w_gate, w_up, w_down):
    return (jax.nn.silu(h @ w_gate) * (h @ w_up)) @ w_down


def token_shift(z):
    return jnp.pad(z, ((0, 0), (1, 0), (0, 0)))[:, :-1]


def apply_rope(x, cos, sin):
    x1, x2 = jnp.split(x.astype(jnp.float32), 2, axis=-1)
    return jnp.concatenate([x1 * cos - x2 * sin, x2 * cos + x1 * sin], axis=-1).astype(x.dtype)


def wkv7_scan(r, decay, k, v, kk, a):
    B, S, H, N = r.shape

    def step(state, inp):
        r_t, w_t, k_t, v_t, kk_t, a_t = inp
        sa = jnp.einsum('bhvk,bhk->bhv', state, -kk_t)
        state = (state * w_t[:, :, None, :]
                 + sa[..., None] * (kk_t * a_t)[:, :, None, :]
                 + v_t[..., None] * k_t[:, :, None, :])
        y_t = jnp.einsum('bhvk,bhk->bhv', state, r_t)
        return state, y_t

    xs = tuple(jnp.moveaxis(t, 1, 0) for t in (r, decay, k, v, kk, a))
    s0 = jnp.zeros((B, H, N, N), jnp.float32)
    _, y = lax.scan(step, s0, xs)
    return jnp.moveaxis(y, 0, 1)


def rwkv7_time_mix(z, mu, w0, w2, a0, a2, g2, k_k, k_a, r_k, gn_w, gn_b):
    B, S, _ = z.shape
    z = z + (token_shift(z) - z) * mu
    r, k, v, wd, ad, gd = jnp.split(
        z, [RWKV_DIM, 2 * RWKV_DIM, 3 * RWKV_DIM, 3 * RWKV_DIM + DECAY_LORA,
            3 * RWKV_DIM + DECAY_LORA + ICLR_LORA], axis=-1)
    w = -jax.nn.softplus(-(w0 + jnp.tanh(wd) @ w2)) - 0.5
    a = jax.nn.sigmoid(a0 + ad @ a2)
    g = jax.nn.sigmoid(gd) @ g2
    kk = k * k_k
    k = k * (1.0 + (a - 1.0) * k_a)

    def heads(t):
        return t.reshape(B, S, RWKV_HEADS, HEAD_DIM).astype(jnp.float32)

    kk = heads(kk)
    kk = kk * lax.rsqrt(jnp.maximum(jnp.sum(kk * kk, axis=-1, keepdims=True), 1e-24))
    decay = jnp.exp(-jnp.exp(heads(w)))
    r_h, k_h, v_h, a_h = heads(r), heads(k), heads(v), heads(a)
    y = wkv7_scan(r_h, decay, k_h, v_h, kk, a_h)
    mean = jnp.mean(y, axis=-1, keepdims=True)
    var = jnp.mean(jnp.square(y - mean), axis=-1, keepdims=True)
    y = ((y - mean) * lax.rsqrt(var + GN_EPS)).reshape(B, S, RWKV_DIM)
    y = y * gn_w.astype(jnp.float32) + gn_b.astype(jnp.float32)
    bonus = jnp.sum(r_h * k_h * r_k.astype(jnp.float32), axis=-1, keepdims=True) * v_h
    y = y + bonus.reshape(B, S, RWKV_DIM)
    return (y * g.astype(jnp.float32)).astype(z.dtype)


def banded_causal_attention(q, k, v, band):
    N, L, H, Dh = q.shape
    nb = -(-L // BAND_BLOCK)
    Lp = nb * BAND_BLOCK
    pad = Lp - L
    qb = jnp.pad(q, ((0, 0), (0, pad), (0, 0), (0, 0))).reshape(N, nb, BAND_BLOCK, H, Dh)

    def key_blocks(t):
        tp = jnp.pad(t, ((0, 0), (BAND_BLOCK, pad), (0, 0), (0, 0))).reshape(N, nb + 1, BAND_BLOCK, H, Dh)
        return jnp.concatenate([tp[:, :-1], tp[:, 1:]], axis=2)

    kb, vb = key_blocks(k), key_blocks(v)
    s = jnp.einsum('nbqhd,nbkhd->nbhqk', qb.astype(jnp.float32), kb.astype(jnp.float32)) * (Dh ** -0.5)
    blk = jnp.arange(nb)[:, None]
    qpos = blk * BAND_BLOCK + jnp.arange(BAND_BLOCK)[None, :]
    kpos = blk * BAND_BLOCK - BAND_BLOCK + jnp.arange(2 * BAND_BLOCK)[None, :]
    dist = qpos[:, :, None] - kpos[:, None, :]
    valid = (dist >= 0) & (dist <= band) & (kpos[:, None, :] >= 0)
    s = jnp.where(valid[None, :, None], s, NEG_INF)
    m = jnp.max(s, axis=-1, keepdims=True)
    e = jnp.exp(s - m)
    l = jnp.sum(e, axis=-1, keepdims=True)
    o = jnp.einsum('nbhqk,nbkhd->nbqhd', e, vb.astype(jnp.float32)) / jnp.swapaxes(l, 2, 3)
    lse = jnp.swapaxes((m + jnp.log(l))[..., 0], 2, 3)
    return o.reshape(N, Lp, H, Dh)[:, :L], lse.reshape(N, Lp, H)[:, :L]


def dilated_causal_attention(q, k, v, window, dilation):
    B, S, H, Dh = q.shape
    L = S // dilation

    def fold(t):
        return t.reshape(B, L, dilation, H, Dh).transpose(0, 2, 1, 3, 4).reshape(B * dilation, L, H, Dh)

    o, lse = banded_causal_attention(fold(q), fold(k), fold(v), window // dilation)
    o = o.reshape(B, dilation, L, H, Dh).transpose(0, 2, 1, 3, 4).reshape(B, S, H, Dh)
    lse = lse.reshape(B, dilation, L, H).transpose(0, 2, 1, 3).reshape(B, S, H)
    return o, lse


def dilated_attention_mix(z, cos, sin, q_gain, k_gain):
    B, S, _ = z.shape
    q, k, v = jnp.split(z, 3, axis=-1)
    q = q.reshape(B, S, ATTN_HEADS, HEAD_DIM)
    k = k.reshape(B, S, ATTN_HEADS, HEAD_DIM)
    v = v.reshape(B, S, ATTN_HEADS, HEAD_DIM)
    q = apply_rope(rms_norm(q, q_gain), cos, sin)
    k = apply_rope(rms_norm(k, k_gain), cos, sin)
    outs, lses = [], []
    for gi, (window, dilation) in enumerate(ATTN_GROUPS):
        hs = slice(gi * HEADS_PER_GROUP, (gi + 1) * HEADS_PER_GROUP)
        o, lse = dilated_causal_attention(q[:, :, hs], k[:, :, hs], v[:, :, hs], window, dilation)
        outs.append(o)
        lses.append(lse)
    wts = jax.nn.softmax(jnp.stack(lses, axis=0), axis=0)
    o = jnp.sum(wts[..., None] * jnp.stack(outs, axis=0), axis=0)
    return o.reshape(B, S, ATTN_OUT_DIM).astype(z.dtype)


def hybrid_layer(x, p_i, cos, sin, ffn1_norm, ffn1_w_gate, ffn1_w_up, ffn1_w_down, mix_norm, w_in,
                 rwkv_mu, rwkv_w0, rwkv_w2, rwkv_a0, rwkv_a2, rwkv_g2, rwkv_k_k, rwkv_k_a, rwkv_r_k,
                 rwkv_gn_w, rwkv_gn_b, q_norm, k_norm, w_br_rwkv, w_br_attn, w_out,
                 ffn2_norm, ffn2_w_gate, ffn2_w_up, ffn2_w_down, ple_norm, ple_w_gate, ple_w_proj):
    x = x + 0.5 * swiglu(rms_norm(x, ffn1_norm), ffn1_w_gate, ffn1_w_up, ffn1_w_down)
    h = rms_norm(x, mix_norm)
    z = h @ w_in
    z_rwkv, z_attn, z_gate = jnp.split(z, [RWKV_COLS, RWKV_COLS + ATTN_COLS], axis=-1)
    y_rwkv = rwkv7_time_mix(z_rwkv, rwkv_mu, rwkv_w0, rwkv_w2, rwkv_a0, rwkv_a2, rwkv_g2,
                            rwkv_k_k, rwkv_k_a, rwkv_r_k, rwkv_gn_w, rwkv_gn_b)
    y_attn = dilated_attention_mix(z_attn, cos, sin, q_norm, k_norm)
    g_rwkv, g_attn = jnp.split(jax.nn.sigmoid(z_gate), N_BRANCHES, axis=-1)
    merged = g_rwkv * (y_rwkv @ w_br_rwkv) + g_attn * (y_attn @ w_br_attn)
    x = x + merged @ w_out
    x = x + 0.5 * swiglu(rms_norm(x, ffn2_norm), ffn2_w_gate, ffn2_w_up, ffn2_w_down)
    x = x + jax.nn.sigmoid(rms_norm(x, ple_norm) @ ple_w_gate) * (p_i @ ple_w_proj)
    return x


def setup_inputs(seed: int = 0) -> dict:
    key = jax.random.key(seed)
    ks = jax.random.split(key, 40)
    f32 = jnp.float32

    def nrm(i, shape, scale):
        return jax.random.normal(ks[i], shape, f32) * scale

    def gain(i, shape):
        return 1.0 + 0.02 * jax.random.normal(ks[i], shape, f32)

    L = DEPTH
    return {
        'x': nrm(0, (BATCH, SEQ, D_MODEL), 1.0),
        'p': nrm(1, (DEPTH, BATCH, SEQ, PLE_DIM), 1.0),
        'positions': (jnp.arange(SEQ, dtype=jnp.int32)[None, :]
                      + jax.random.randint(ks[2], (BATCH, 1), 0, 4096, dtype=jnp.int32)),
        'ffn1_norm': gain(3, (L, D_MODEL)),
        'ffn1_w_gate': nrm(4, (L, D_MODEL, D_FF), D_MODEL ** -0.5),
        'ffn1_w_up': nrm(5, (L, D_MODEL, D_FF), D_MODEL ** -0.5),
        'ffn1_w_down': nrm(6, (L, D_FF, D_MODEL), D_FF ** -0.5),
        'mix_norm': gain(7, (L, D_MODEL)),
        'w_in': nrm(8, (L, D_MODEL, IN_COLS), D_MODEL ** -0.5),
        'rwkv_mu': jax.random.uniform(ks[9], (L, RWKV_COLS), f32),
        'rwkv_w0': -0.5 - 4.5 * jax.random.uniform(ks[10], (L, RWKV_DIM), f32),
        'rwkv_w2': nrm(11, (L, DECAY_LORA, RWKV_DIM), 0.1),
        'rwkv_a0': nrm(12, (L, RWKV_DIM), 0.1),
        'rwkv_a2': nrm(13, (L, ICLR_LORA, RWKV_DIM), 0.1),
        'rwkv_g2': nrm(14, (L, GATE_LORA, RWKV_DIM), GATE_LORA ** -0.5),
        'rwkv_k_k': 0.85 + nrm(15, (L, RWKV_DIM), 0.02),
        'rwkv_k_a': 1.0 + nrm(16, (L, RWKV_DIM), 0.02),
        'rwkv_r_k': nrm(17, (L, RWKV_HEADS, HEAD_DIM), 0.1),
        'rwkv_gn_w': gain(18, (L, RWKV_DIM)),
        'rwkv_gn_b': nrm(19, (L, RWKV_DIM), 0.01),
        'q_norm': gain(20, (L, HEAD_DIM)),
        'k_norm': gain(21, (L, HEAD_DIM)),
        'w_br_rwkv': nrm(22, (L, RWKV_DIM, D_MODEL), RWKV_DIM ** -0.5),
        'w_br_attn': nrm(23, (L, ATTN_OUT_DIM, D_MODEL), ATTN_OUT_DIM ** -0.5),
        'w_out': nrm(24, (L, D_MODEL, D_MODEL), D_MODEL ** -0.5),
        'ffn2_norm': gain(25, (L, D_MODEL)),
        'ffn2_w_gate': nrm(26, (L, D_MODEL, D_FF), D_MODEL ** -0.5),
        'ffn2_w_up': nrm(27, (L, D_MODEL, D_FF), D_MODEL ** -0.5),
        'ffn2_w_down': nrm(28, (L, D_FF, D_MODEL), D_FF ** -0.5),
        'ple_norm': gain(29, (L, D_MODEL)),
        'ple_w_gate': nrm(30, (L, D_MODEL, D_MODEL), D_MODEL ** -0.5),
        'ple_w_proj': nrm(31, (L, PLE_DIM, D_MODEL), PLE_DIM ** -0.5),
    }


def reference(x, p, positions, ffn1_norm, ffn1_w_gate, ffn1_w_up, ffn1_w_down, mix_norm, w_in,
              rwkv_mu, rwkv_w0, rwkv_w2, rwkv_a0, rwkv_a2, rwkv_g2, rwkv_k_k, rwkv_k_a, rwkv_r_k,
              rwkv_gn_w, rwkv_gn_b, q_norm, k_norm, w_br_rwkv, w_br_attn, w_out,
              ffn2_norm, ffn2_w_gate, ffn2_w_up, ffn2_w_down, ple_norm, ple_w_gate, ple_w_proj):
    inv_freq = 1.0 / (ROPE_THETA ** (jnp.arange(0, HEAD_DIM, 2, dtype=jnp.float32) / HEAD_DIM))
    ang = positions.astype(jnp.float32)[..., None] * inv_freq
    cos = jnp.cos(ang)[:, :, None, :]
    sin = jnp.sin(ang)[:, :, None, :]
    for i in range(DEPTH):
        x = hybrid_layer(x, p[i], cos, sin, ffn1_norm[i], ffn1_w_gate[i], ffn1_w_up[i], ffn1_w_down[i],
                         mix_norm[i], w_in[i], rwkv_mu[i], rwkv_w0[i], rwkv_w2[i], rwkv_a0[i], rwkv_a2[i],
                         rwkv_g2[i], rwkv_k_k[i], rwkv_k_a[i], rwkv_r_k[i], rwkv_gn_w[i], rwkv_gn_b[i],
                         q_norm[i], k_norm[i], w_br_rwkv[i], w_br_attn[i], w_out[i],
                         ffn2_norm[i], ffn2_w_gate[i], ffn2_w_up[i], ffn2_w_down[i],
                         ple_norm[i], ple_w_gate[i], ple_w_proj[i])
    return x
```

```python
import functools
import math

import numpy as np
import jax
import jax.numpy as jnp
from jax import lax
from jax.experimental import pallas as pl
from jax.experimental.pallas import tpu as pltpu

F32 = jnp.float32
BF16 = jnp.bfloat16

D_MODEL = 1024
PLE_DIM = 256
HEAD_DIM = 64
RWKV_HEADS = 8
RWKV_DIM = RWKV_HEADS * HEAD_DIM
DECAY_LORA = 64
ICLR_LORA = 64
GATE_LORA = 128
GN_EPS = 64e-5
ATTN_GROUPS = ((128, 1), (512, 4), (2048, 16))
HEADS_PER_GROUP = 4
ATTN_DIM = HEADS_PER_GROUP * len(ATTN_GROUPS) * HEAD_DIM
ATTN_OUT_DIM = HEADS_PER_GROUP * HEAD_DIM
BAND_BLOCK = 128
ROPE_THETA = 10000.0
NEG_INF = -1e30
D_FF = 2816
RMS_EPS = 1e-6
RWKV_COLS = 3 * RWKV_DIM + DECAY_LORA + ICLR_LORA + GATE_LORA
ATTN_COLS = 3 * ATTN_DIM
GATE_COLS = 2 * D_MODEL
GROUP_COLS = 3 * ATTN_OUT_DIM

LANES = 128
SUBLANES = 8
VMEM_LIMIT_BYTES = 60 * 1024 * 1024

FFN_TM = 512
OUT_TM = 256
RWKV_TS = 256
CHUNK = 64
ATTN_ROWS = 256


def _dot(a, b):
    return jnp.dot(a, b, preferred_element_type=F32)


def _dot_nt(a, b):
    return lax.dot_general(a, b, (((1,), (1,)), ((), ())), preferred_element_type=F32)


def _dot_tn(a, b):
    return lax.dot_general(a, b, (((0,), (0,)), ((), ())), preferred_element_type=F32)


def _rms(x, gain):
    return x * lax.rsqrt(jnp.mean(x * x, axis=-1, keepdims=True) + RMS_EPS) * gain


def _sigmoid(x):
    return 1.0 / (1.0 + jnp.exp(-x))


def _segsum(x, seg):
    hi = x.astype(BF16)
    lo = (x - hi.astype(F32)).astype(BF16)
    return _dot(hi, seg) + _dot(lo, seg)


def _const_spec(arr):
    nd = arr.ndim
    return pl.BlockSpec(arr.shape, lambda *_: (0,) * nd, pipeline_mode=pl.Buffered(1))


def _params(*sem):
    return pltpu.CompilerParams(dimension_semantics=sem, vmem_limit_bytes=VMEM_LIMIT_BYTES)


def _ffn1_kernel(x_ref, n1_ref, wg_ref, wu_ref, wd_ref, n2_ref, x1_ref, h2_ref):
    x = x_ref[...]
    h = _rms(x, n1_ref[...]).astype(BF16)
    g = _dot(h, wg_ref[...])
    u = _dot(h, wu_ref[...])
    act = (g * _sigmoid(g) * u).astype(BF16)
    x1 = x + 0.5 * _dot(act, wd_ref[...])
    x1_ref[...] = x1
    h2_ref[...] = _rms(x1, n2_ref[...]).astype(BF16)


def _ffn1(x2d, n1, wg, wu, wd, n2):
    T = x2d.shape[0]
    row = lambda i: (i, 0)
    return pl.pallas_call(
        _ffn1_kernel,
        grid=(T // FFN_TM,),
        in_specs=[pl.BlockSpec((FFN_TM, D_MODEL), row), _const_spec(n1), _const_spec(wg),
                  _const_spec(wu), _const_spec(wd), _const_spec(n2)],
        out_specs=[pl.BlockSpec((FFN_TM, D_MODEL), row), pl.BlockSpec((FFN_TM, D_MODEL), row)],
        out_shape=[jax.ShapeDtypeStruct((T, D_MODEL), F32), jax.ShapeDtypeStruct((T, D_MODEL), BF16)],
        compiler_params=_params("parallel"),
        name="ffn1",
    )(x2d, n1, wg, wu, wd, n2)


def _inproj_kernel(h_ref, w_ref, zr_ref, za_ref, gt_ref):
    h = h_ref[...]
    zr_ref[...] = _dot(h, w_ref[:, :RWKV_COLS])
    za_ref[...] = _dot(h, w_ref[:, RWKV_COLS:RWKV_COLS + ATTN_COLS]).astype(BF16)
    gt_ref[...] = _sigmoid(_dot(h, w_ref[:, RWKV_COLS + ATTN_COLS:])).astype(BF16)


def _inproj(h2, w):
    T = h2.shape[0]
    row = lambda i: (i, 0)
    return pl.pallas_call(
        _inproj_kernel,
        grid=(T // FFN_TM,),
        in_specs=[pl.BlockSpec((FFN_TM, D_MODEL), row), _const_spec(w)],
        out_specs=[pl.BlockSpec((FFN_TM, RWKV_COLS), row), pl.BlockSpec((FFN_TM, ATTN_COLS), row),
                   pl.BlockSpec((FFN_TM, GATE_COLS), row)],
        out_shape=[jax.ShapeDtypeStruct((T, RWKV_COLS), F32), jax.ShapeDtypeStruct((T, ATTN_COLS), BF16),
                   jax.ShapeDtypeStruct((T, GATE_COLS), BF16)],
        compiler_params=_params("parallel"),
        name="inproj",
    )(h2, w)


def _scan_chunk_pair(rows, lanes, h_ref, pr, r_s, k_s, v_s, na_s, b_s, lw_s, y_s):
    C = CHUNK
    C2 = 2 * C
    lw = lw_s[rows, lanes]
    ri64 = lax.broadcasted_iota(jnp.int32, (C, C), 0)
    ci64 = lax.broadcasted_iota(jnp.int32, (C, C), 1)
    tril = jnp.where(ci64 <= ri64, 1.0, 0.0).astype(F32)
    cum = jnp.dot(tril, lw, precision=lax.Precision.HIGHEST, preferred_element_type=F32)
    p_inc = jnp.exp(cum)
    p_exc = jnp.exp(cum - lw)
    p_inv = jnp.exp(-cum)
    p_end = p_inc[C - 1:C, :]
    to_end = p_inv * p_end

    na = na_s[rows, lanes]
    r = r_s[rows, lanes]
    b = b_s[rows, lanes]
    k = k_s[rows, lanes]
    v = v_s[rows, lanes]

    first = lax.broadcasted_iota(jnp.int32, (C, LANES), 1) < HEAD_DIM

    def stack(x):
        return jnp.concatenate([jnp.where(first, x, 0.0), jnp.where(first, 0.0, x)], axis=0)

    a_st = stack(na * p_exc)
    r_st = stack(r * p_inc)
    v_st = stack(v)
    v_bf = v_st.astype(BF16)
    xr = jnp.concatenate([a_st, r_st], axis=0).astype(BF16)
    y_g = jnp.concatenate([stack(b * p_inv), stack(k * p_inv)], axis=0).astype(BF16)
    y_h = jnp.concatenate([stack(b * to_end), stack(k * to_end)], axis=0).astype(BF16)

    g = _dot_nt(xr, y_g)
    ri = lax.broadcasted_iota(jnp.int32, (C2, C2), 0)
    ci = lax.broadcasted_iota(jnp.int32, (C2, C2), 1)
    strict = ci < ri
    incl = ci <= ri
    diag = ci == ri
    l_ab = jnp.where(strict, g[:C2, :C2], 0.0)
    l_ak = jnp.where(strict, g[:C2, C2:], 0.0)
    m_rb = jnp.where(incl, g[C2:, :C2], 0.0)
    m_rk = jnp.where(incl, g[C2:, C2:], 0.0)

    l_bf = l_ab.astype(BF16)
    pw = _dot(l_bf, l_bf)
    acc = jnp.where(diag, 1.0, 0.0) + l_ab
    levels = int(math.log2(C))
    for _ in range(1, levels - 1):
        ps = _dot(pw.astype(BF16), jnp.concatenate([acc, pw], axis=1).astype(BF16))
        acc = acc + ps[:, :C2]
        pw = ps[:, C2:]
    acc = acc + _dot(pw.astype(BF16), acc.astype(BF16))
    t_bf = acc.astype(BF16)

    lm = _dot(jnp.concatenate([l_ak, m_rk], axis=0).astype(BF16), v_bf)
    av = _dot(t_bf, jnp.concatenate([a_st, lm[:C2]], axis=1).astype(BF16))
    av_bf = av.astype(BF16)
    ry = _dot(m_rb.astype(BF16), av_bf) + jnp.concatenate([r_st, lm[C2:]], axis=1)
    zv = jnp.concatenate([jnp.zeros((C2, C2), BF16), v_bf], axis=1)
    md = _dot_tn(y_h, jnp.concatenate([av_bf, zv], axis=0))
    m_p = jnp.where(diag, jnp.broadcast_to(p_end, (C2, C2)), 0.0) + md[:, :C2]

    h0 = h_ref[pr]
    rm = _dot(jnp.concatenate([ry[:, :C2], m_p], axis=0).astype(BF16), h0.astype(BF16))
    y_st = rm[:C2] + ry[:, C2:]
    h_ref[pr] = rm[C2:] + md[:, C2:]
    y_s[rows, lanes] = y_st[:C] + y_st[C:]


def _rwkv_kernel(z_ref, zh_ref, mu_ref, w0_ref, a0_ref, w2a_ref, g2_ref, kk_ref, ka_ref, rk_ref,
                 gnw_ref, gnb_ref, seg_ref, y_ref,
                 h_ref, r_s, k_s, v_s, na_s, b_s, lw_s, g_s, y_s):
    t = pl.program_id(1)
    TS = RWKV_TS
    D = RWKV_DIM

    @pl.when(t == 0)
    def _():
        h_ref[...] = jnp.zeros_like(h_ref)

    has_prev = jnp.where(t > 0, 1.0, 0.0).astype(F32)

    def shifted(lo, hi):
        zc = z_ref[:, lo:hi]
        prev = zh_ref[SUBLANES - 1:SUBLANES, lo:hi] * has_prev
        row = lax.broadcasted_iota(jnp.int32, zc.shape, 0)
        zp = jnp.where(row == 0, prev, pltpu.roll(zc, 1, 0))
        return zc + (zp - zc) * mu_ref[:, lo:hi]

    seg = seg_ref[...]
    r = shifted(0, D)
    k = shifted(D, 2 * D)
    v = shifted(2 * D, 3 * D)
    wa = shifted(3 * D, 3 * D + DECAY_LORA + ICLR_LORA)
    gd = shifted(3 * D + DECAY_LORA + ICLR_LORA, RWKV_COLS)

    lane = lax.broadcasted_iota(jnp.int32, wa.shape, 1)
    lora_in = jnp.where(lane < DECAY_LORA, jnp.tanh(wa), wa).astype(BF16)
    lora = _dot(lora_in, w2a_ref[...])
    lw_s[...] = -math.exp(-0.5) * _sigmoid(w0_ref[...] + lora[:, :D])
    a = _sigmoid(a0_ref[...] + lora[:, D:])
    g_s[...] = _dot(_sigmoid(gd).astype(BF16), g2_ref[...])

    kk = k * kk_ref[...]
    kk = kk * lax.rsqrt(jnp.maximum(_segsum(kk * kk, seg), 1e-24))
    k2 = k * (1.0 + (a - 1.0) * ka_ref[...])
    r_s[...] = r
    k_s[...] = k2
    v_s[...] = v
    na_s[...] = -kk
    b_s[...] = kk * a

    def chunk_body(c, carry):
        rows = pl.ds(pl.multiple_of(c * CHUNK, CHUNK), CHUNK)
        for pr in range(RWKV_HEADS // 2):
            _scan_chunk_pair(rows, slice(pr * LANES, (pr + 1) * LANES), h_ref, pr,
                             r_s, k_s, v_s, na_s, b_s, lw_s, y_s)
        return carry

    lax.fori_loop(0, TS // CHUNK, chunk_body, 0)

    y = y_s[...]
    inv_n = 1.0 / HEAD_DIM
    mean = _segsum(y, seg) * inv_n
    yc = y - mean
    var = _segsum(yc * yc, seg) * inv_n
    yn = yc * lax.rsqrt(var + GN_EPS) * gnw_ref[...] + gnb_ref[...]
    bonus = _segsum(r_s[...] * k_s[...] * rk_ref[...], seg) * v_s[...]
    y_ref[...] = ((yn + bonus) * g_s[...]).astype(BF16)


def _rwkv(zr, mu, w0, a0, w2a, g2, k_k, k_a, r_k, gn_w, gn_b, seg, B, S):
    TS = RWKV_TS
    consts = [mu, w0, a0, w2a, g2, k_k, k_a, r_k, gn_w, gn_b, seg]
    halo_blocks = TS // SUBLANES
    stage = pltpu.VMEM((TS, RWKV_DIM), F32)
    return pl.pallas_call(
        _rwkv_kernel,
        grid=(B, S // TS),
        in_specs=[pl.BlockSpec((None, TS, RWKV_COLS), lambda b, t: (b, t, 0)),
                  pl.BlockSpec((None, SUBLANES, RWKV_COLS),
                               lambda b, t: (b, jnp.maximum(t * halo_blocks - 1, 0), 0))]
                 + [_const_spec(c) for c in consts],
        out_specs=pl.BlockSpec((None, TS, RWKV_DIM), lambda b, t: (b, t, 0)),
        out_shape=jax.ShapeDtypeStruct((B, S, RWKV_DIM), BF16),
        scratch_shapes=[pltpu.VMEM((RWKV_HEADS // 2, 2 * CHUNK, 2 * CHUNK), F32)] + [stage] * 8,
        compiler_params=_params("parallel", "arbitrary"),
        name="rwkv",
    )(zr, zr, *consts)


def _attn_group(gi, dil, S, x_s, o_s, l_s):
    L = S // dil
    nb = L // BAND_BLOCK
    Q = BAND_BLOCK
    qq = lax.broadcasted_iota(jnp.int32, (Q, Q), 0)
    kq = lax.broadcasted_iota(jnp.int32, (Q, Q), 1)
    lane2 = lax.broadcasted_iota(jnp.int32, (Q, 2 * LANES), 1)
    half = HEAD_DIM // 2

    def unit(u, carry):
        if nb > 1:
            res = u // nb
            blk = u % nb
        else:
            res = u
            blk = 0
        start = res + blk * (Q * dil)

        def load(slab, st):
            return x_s[slab, pl.ds(st, Q, stride=dil), :]

        def both(slab, st):
            return jnp.concatenate([load(slab, st), load(slab + 1, st)], axis=1).astype(BF16)

        q = both(0, start)
        kcur = both(2, start)
        vcur = both(4, start)
        if nb > 1:
            pstart = res + jnp.maximum(blk - 1, 0) * (Q * dil)
            keys = jnp.concatenate([both(2, pstart), kcur], axis=0)
            vals = jnp.concatenate([both(4, pstart), vcur], axis=0)
            off = jnp.where(blk > 0, 0, 2 * Q)
            valid = jnp.concatenate([kq >= qq + off, kq <= qq], axis=1)
        else:
            keys, vals = kcur, vcur
            valid = kq <= qq

        o = jnp.zeros((Q, 2 * LANES), F32)
        lse = jnp.zeros((Q, 2 * LANES), F32)
        for j in range(HEADS_PER_GROUP):
            qmask = (lane2 % LANES) // half == j
            s = _dot_nt(jnp.where(qmask, q, jnp.zeros_like(q)), keys)
            s = jnp.where(valid, s, NEG_INF)
            m = jnp.max(s, axis=-1, keepdims=True)
            e = jnp.exp(s - m)
            l = jnp.sum(e, axis=-1, keepdims=True)
            oj = _dot(e.astype(BF16), vals) / l
            hmask = lane2 // HEAD_DIM == j
            o = jnp.where(hmask, oj, o)
            lse = jnp.where(hmask, m + jnp.log(l), lse)
        rows = pl.ds(start, Q, stride=dil)
        o_s[gi, 0, rows, :] = o[:, :LANES]
        o_s[gi, 1, rows, :] = o[:, LANES:]
        l_s[gi, 0, rows, :] = lse[:, :LANES]
        l_s[gi, 1, rows, :] = lse[:, LANES:]
        return carry

    lax.fori_loop(0, dil * nb, unit, 0)


def _attn_kernel(za_ref, pos_ref, invf_ref, qga_ref, qgb_ref, kga_ref, kgb_ref, seg_ref, out_ref,
                 x_s, o_s, l_s, cs_s):
    g = pl.program_id(1)
    S = za_ref.shape[0]
    R = ATTN_ROWS

    @pl.when(g == 0)
    def _():
        def body(i, carry):
            rows = pl.ds(pl.multiple_of(i * R, R), R)
            ang = pos_ref[rows, :].astype(F32) * invf_ref[...]
            cs_s[0, rows, :] = jnp.cos(ang)
            cs_s[1, rows, :] = jnp.sin(ang)
            return carry
        lax.fori_loop(0, S // R, body, 0)

    seg = seg_ref[...]

    def prep(i, carry):
        rows = pl.ds(pl.multiple_of(i * R, R), R)
        cos = cs_s[0, rows, :]
        sin = cs_s[1, rows, :]
        for col, slab, ga_ref, gb_ref, scale in ((0, 0, qga_ref, qgb_ref, HEAD_DIM ** -0.5),
                                                 (2 * LANES, 2, kga_ref, kgb_ref, 1.0)):
            xa = za_ref[rows, col:col + LANES].astype(F32)
            xb = za_ref[rows, col + LANES:col + 2 * LANES].astype(F32)
            ms = _segsum(xa * xa + xb * xb, seg) * (1.0 / HEAD_DIM)
            inv = lax.rsqrt(ms + RMS_EPS)
            xa = xa * inv * ga_ref[...]
            xb = xb * inv * gb_ref[...]
            x_s[slab, rows, :] = (xa * cos - xb * sin) * scale
            x_s[slab + 1, rows, :] = (xb * cos + xa * sin) * scale
        x_s[4, rows, :] = za_ref[rows, 4 * LANES:5 * LANES].astype(F32)
        x_s[5, rows, :] = za_ref[rows, 5 * LANES:6 * LANES].astype(F32)
        return carry

    lax.fori_loop(0, S // R, prep, 0)

    for gi, (_, dil) in enumerate(ATTN_GROUPS):
        pl.when(g == gi)(functools.partial(_attn_group, gi, dil, S, x_s, o_s, l_s))

    @pl.when(g == len(ATTN_GROUPS) - 1)
    def _():
        def body(i, carry):
            rows = pl.ds(pl.multiple_of(i * R, R), R)
            for slab in range(2):
                ls = [l_s[gi, slab, rows, :] for gi in range(len(ATTN_GROUPS))]
                m = jnp.maximum(jnp.maximum(ls[0], ls[1]), ls[2])
                ws = [jnp.exp(x - m) for x in ls]
                num = sum(w * o_s[gi, slab, rows, :] for gi, w in enumerate(ws))
                out_ref[rows, slab * LANES:(slab + 1) * LANES] = (num / (ws[0] + ws[1] + ws[2])).astype(BF16)
            return carry
        lax.fori_loop(0, S // R, body, 0)


def _attn(za, pos, invf, qga, qgb, kga, kgb, seg32, B, S):
    consts = [invf, qga, qgb, kga, kgb, seg32]
    ng = len(ATTN_GROUPS)
    return pl.pallas_call(
        _attn_kernel,
        grid=(B, ng),
        in_specs=[pl.BlockSpec((None, S, GROUP_COLS), lambda b, g: (b, 0, g)),
                  pl.BlockSpec((None, S, 1), lambda b, g: (b, 0, 0))]
                 + [_const_spec(c) for c in consts],
        out_specs=pl.BlockSpec((None, S, ATTN_OUT_DIM), lambda b, g: (b, 0, 0)),
        out_shape=jax.ShapeDtypeStruct((B, S, ATTN_OUT_DIM), BF16),
        scratch_shapes=[pltpu.VMEM((6, S, LANES), F32), pltpu.VMEM((ng, 2, S, LANES), F32),
                        pltpu.VMEM((ng, 2, S, LANES), F32), pltpu.VMEM((2, S, LANES), F32)],
        compiler_params=_params("parallel", "arbitrary"),
        name="attn",
    )(za, pos, *consts)


def _out_kernel(x1_ref, gt_ref, yr_ref, ya_ref, p_ref, wbr_ref, wba_ref, wo_ref, n2_ref, wg_ref, wu_ref,
                wd_ref, pn_ref, pwg_ref, pwp_ref, out_ref):
    br = _dot(yr_ref[...], wbr_ref[...])
    ba = _dot(ya_ref[...], wba_ref[...])
    merged = gt_ref[:, :D_MODEL].astype(F32) * br + gt_ref[:, D_MODEL:].astype(F32) * ba
    x2 = x1_ref[...] + _dot(merged.astype(BF16), wo_ref[...])
    h = _rms(x2, n2_ref[...]).astype(BF16)
    g = _dot(h, wg_ref[...])
    u = _dot(h, wu_ref[...])
    act = (g * _sigmoid(g) * u).astype(BF16)
    x3 = x2 + 0.5 * _dot(act, wd_ref[...])
    hp = _rms(x3, pn_ref[...]).astype(BF16)
    gate = _sigmoid(_dot(hp, pwg_ref[...]))
    out_ref[...] = x3 + gate * _dot(p_ref[...].astype(BF16), pwp_ref[...])


def _out(x1, gt, yr, ya, p2d, consts):
    T = x1.shape[0]
    TM = OUT_TM
    row = lambda i: (i, 0)
    return pl.pallas_call(
        _out_kernel,
        grid=(T // TM,),
        in_specs=[pl.BlockSpec((TM, D_MODEL), row), pl.BlockSpec((TM, GATE_COLS), row),
                  pl.BlockSpec((TM, RWKV_DIM), row), pl.BlockSpec((TM, ATTN_OUT_DIM), row),
                  pl.BlockSpec((TM, PLE_DIM), row)] + [_const_spec(c) for c in consts],
        out_specs=pl.BlockSpec((TM, D_MODEL), row),
        out_shape=jax.ShapeDtypeStruct((T, D_MODEL), F32),
        compiler_params=_params("parallel"),
        name="merge_ffn2",
    )(x1, gt, yr, ya, p2d, *consts)


def _attn_column_order():
    half = HEAD_DIM // 2
    order = []
    for g in range(len(ATTN_GROUPS)):
        heads = [g * HEADS_PER_GROUP + j for j in range(HEADS_PER_GROUP)]
        for base in (0, ATTN_DIM):
            for lo in (0, half):
                for h in heads:
                    order.extend(base + h * HEAD_DIM + lo + d for d in range(half))
        for h in heads:
            order.extend(2 * ATTN_DIM + h * HEAD_DIM + d for d in range(HEAD_DIM))
    return np.asarray(order, np.int32)


def _block_ones(n, blk):
    idx = np.arange(n) // blk
    return jnp.asarray(idx[:, None] == idx[None, :], BF16)


def kernel(x, p, positions, ffn1_norm, ffn1_w_gate, ffn1_w_up, ffn1_w_down, mix_norm, w_in, rwkv_mu, rwkv_w0, rwkv_w2, rwkv_a0, rwkv_a2, rwkv_g2, rwkv_k_k, rwkv_k_a, rwkv_r_k, rwkv_gn_w, rwkv_gn_b, q_norm, k_norm, w_br_rwkv, w_br_attn, w_out, ffn2_norm, ffn2_w_gate, ffn2_w_up, ffn2_w_down, ple_norm, ple_w_gate, ple_w_proj):
    B, S, D = x.shape
    depth = p.shape[0]
    T = B * S
    half = HEAD_DIM // 2
    bf = lambda w: w.astype(BF16)
    rowvec = lambda a: a.reshape(1, -1).astype(F32)

    inv_freq = 1.0 / (ROPE_THETA ** (jnp.arange(0, HEAD_DIM, 2, dtype=F32) / HEAD_DIM))
    invf = jnp.tile(inv_freq, HEADS_PER_GROUP).reshape(1, LANES)
    pos = positions.reshape(B, S, 1)
    seg64 = _block_ones(RWKV_DIM, HEAD_DIM)
    seg32 = _block_ones(LANES, half)
    attn_cols = RWKV_COLS + _attn_column_order()

    xc = x.reshape(T, D)
    for i in range(depth):
        w_in_i = w_in[i]
        w_in_perm = jnp.concatenate(
            [w_in_i[:, :RWKV_COLS], w_in_i[:, attn_cols], w_in_i[:, RWKV_COLS + ATTN_COLS:]], axis=1)
        zeros = jnp.zeros((DECAY_LORA, RWKV_DIM), F32)
        w2a = jnp.concatenate([jnp.concatenate([rwkv_w2[i], zeros], axis=1),
                               jnp.concatenate([zeros, rwkv_a2[i]], axis=1)], axis=0)

        x1, h2 = _ffn1(xc, rowvec(ffn1_norm[i]), bf(ffn1_w_gate[i]), bf(ffn1_w_up[i]), bf(ffn1_w_down[i]),
                       rowvec(mix_norm[i]))
        zr, za, gt = _inproj(h2, bf(w_in_perm))
        yr = _rwkv(zr.reshape(B, S, RWKV_COLS), rowvec(rwkv_mu[i]), rowvec(rwkv_w0[i]), rowvec(rwkv_a0[i]),
                   bf(w2a), bf(rwkv_g2[i]), rowvec(rwkv_k_k[i]), rowvec(rwkv_k_a[i]), rowvec(rwkv_r_k[i]),
                   rowvec(rwkv_gn_w[i]), rowvec(rwkv_gn_b[i]), seg64, B, S)
        tile4 = lambda a: jnp.tile(a, HEADS_PER_GROUP).reshape(1, LANES).astype(F32)
        ya = _attn(za.reshape(B, S, ATTN_COLS), pos, invf, tile4(q_norm[i][:half]), tile4(q_norm[i][half:]),
                   tile4(k_norm[i][:half]), tile4(k_norm[i][half:]), seg32, B, S)
        consts = [bf(w_br_rwkv[i]), bf(w_br_attn[i]), bf(w_out[i]), rowvec(ffn2_norm[i]), bf(ffn2_w_gate[i]),
                  bf(ffn2_w_up[i]), bf(ffn2_w_down[i]), rowvec(ple_norm[i]), bf(ple_w_gate[i]),
                  bf(ple_w_proj[i])]
        xc = _out(x1, gt, yr.reshape(T, RWKV_DIM), ya.reshape(T, ATTN_OUT_DIM), p[i].reshape(T, PLE_DIM), consts)
    return xc.reshape(B, S, D)
```

```python
import functools
import math

import numpy as np
import jax
import jax.numpy as jnp
from jax import lax
from jax.experimental import pallas as pl
from jax.experimental.pallas import tpu as pltpu

F32 = jnp.float32
BF16 = jnp.bfloat16

D_MODEL = 1024
PLE_DIM = 256
HEAD_DIM = 64
RWKV_HEADS = 8
RWKV_DIM = RWKV_HEADS * HEAD_DIM
DECAY_LORA = 64
ICLR_LORA = 64
GATE_LORA = 128
GN_EPS = 64e-5
ATTN_GROUPS = ((128, 1), (512, 4), (2048, 16))
HEADS_PER_GROUP = 4
ATTN_DIM = HEADS_PER_GROUP * len(ATTN_GROUPS) * HEAD_DIM
ATTN_OUT_DIM = HEADS_PER_GROUP * HEAD_DIM
BAND_BLOCK = 128
ROPE_THETA = 10000.0
NEG_INF = -1e30
D_FF = 2816
RMS_EPS = 1e-6
RWKV_COLS = 3 * RWKV_DIM + DECAY_LORA + ICLR_LORA + GATE_LORA
ATTN_COLS = 3 * ATTN_DIM
GATE_COLS = 2 * D_MODEL
GROUP_COLS = 3 * ATTN_OUT_DIM

LANES = 128
SUBLANES = 8
VMEM_LIMIT_BYTES = 60 * 1024 * 1024

FFN_TM = 512
OUT_TM = 256
RWKV_TS = 256
CHUNK = 64
CHUNKS_PER_ITER = 4
ATTN_ROWS = 256


def _dot(a, b):
    return jnp.dot(a, b, preferred_element_type=F32)


def _dot_nt(a, b):
    return lax.dot_general(a, b, (((1,), (1,)), ((), ())), preferred_element_type=F32)


def _dot_tn(a, b):
    return lax.dot_general(a, b, (((0,), (0,)), ((), ())), preferred_element_type=F32)


def _rms(x, gain):
    return x * lax.rsqrt(jnp.mean(x * x, axis=-1, keepdims=True) + RMS_EPS) * gain


def _sigmoid(x):
    return 1.0 / (1.0 + jnp.exp(-x))


def _segsum(x, seg):
    w = seg.shape[0]
    parts = [_dot(x[:, lo:lo + w].astype(BF16), seg) for lo in range(0, x.shape[1], w)]
    return parts[0] if len(parts) == 1 else jnp.concatenate(parts, axis=1)


def _const_spec(arr):
    nd = arr.ndim
    return pl.BlockSpec(arr.shape, lambda *_: (0,) * nd, pipeline_mode=pl.Buffered(1))


def _params(*sem):
    return pltpu.CompilerParams(dimension_semantics=sem, vmem_limit_bytes=VMEM_LIMIT_BYTES)


def _ffn1_kernel(x_ref, n1_ref, wg_ref, wu_ref, wd_ref, n2_ref, x1_ref, h2_ref):
    x = x_ref[...]
    h = _rms(x, n1_ref[...]).astype(BF16)
    g = _dot(h, wg_ref[...])
    u = _dot(h, wu_ref[...])
    act = (g * _sigmoid(g) * u).astype(BF16)
    x1 = x + 0.5 * _dot(act, wd_ref[...])
    x1_ref[...] = x1
    h2_ref[...] = _rms(x1, n2_ref[...]).astype(BF16)


def _ffn1(x2d, n1, wg, wu, wd, n2):
    T = x2d.shape[0]
    row = lambda i: (i, 0)
    return pl.pallas_call(
        _ffn1_kernel,
        grid=(T // FFN_TM,),
        in_specs=[pl.BlockSpec((FFN_TM, D_MODEL), row), _const_spec(n1), _const_spec(wg),
                  _const_spec(wu), _const_spec(wd), _const_spec(n2)],
        out_specs=[pl.BlockSpec((FFN_TM, D_MODEL), row), pl.BlockSpec((FFN_TM, D_MODEL), row)],
        out_shape=[jax.ShapeDtypeStruct((T, D_MODEL), F32), jax.ShapeDtypeStruct((T, D_MODEL), BF16)],
        compiler_params=_params("parallel"),
        name="ffn1",
    )(x2d, n1, wg, wu, wd, n2)


def _inproj_kernel(h_ref, w_ref, zr_ref, za_ref, gt_ref):
    h = h_ref[...]
    zr_ref[...] = _dot(h, w_ref[:, :RWKV_COLS])
    za_ref[...] = _dot(h, w_ref[:, RWKV_COLS:RWKV_COLS + ATTN_COLS]).astype(BF16)
    gt_ref[...] = _sigmoid(_dot(h, w_ref[:, RWKV_COLS + ATTN_COLS:])).astype(BF16)


def _inproj(h2, w):
    T = h2.shape[0]
    row = lambda i: (i, 0)
    return pl.pallas_call(
        _inproj_kernel,
        grid=(T // FFN_TM,),
        in_specs=[pl.BlockSpec((FFN_TM, D_MODEL), row), _const_spec(w)],
        out_specs=[pl.BlockSpec((FFN_TM, RWKV_COLS), row), pl.BlockSpec((FFN_TM, ATTN_COLS), row),
                   pl.BlockSpec((FFN_TM, GATE_COLS), row)],
        out_shape=[jax.ShapeDtypeStruct((T, RWKV_COLS), F32), jax.ShapeDtypeStruct((T, ATTN_COLS), BF16),
                   jax.ShapeDtypeStruct((T, GATE_COLS), BF16)],
        compiler_params=_params("parallel"),
        name="inproj",
    )(h2, w)


def _scan_chunks(row_list, h_ref, r_s, k_s, v_s, na_s, b_s, lw_s, cum_s, y_s):
    C = CHUNK
    C2 = 2 * C
    pairs = range(RWKV_HEADS // 2)
    rows_of = [rows for rows in row_list for _ in pairs]
    lanes = [slice(pr * LANES, (pr + 1) * LANES) for _ in row_list for pr in pairs]
    first = lax.broadcasted_iota(jnp.int32, (C, LANES), 1) < HEAD_DIM
    ri = lax.broadcasted_iota(jnp.int32, (C2, C2), 0)
    ci = lax.broadcasted_iota(jnp.int32, (C2, C2), 1)
    strict = ci < ri
    incl = ci <= ri
    diag = ci == ri
    eye = jnp.where(diag, 1.0, 0.0)

    def stack(x):
        return jnp.concatenate([jnp.where(first, x, 0.0), jnp.where(first, 0.0, x)], axis=0)

    def cat(parts, axis):
        return jnp.concatenate(parts, axis=axis)

    def load(ref):
        return [ref[rows, ln] for rows, ln in zip(rows_of, lanes)]

    lw = load(lw_s)
    cum = load(cum_s)
    p_inc = [jnp.exp(x) for x in cum]
    p_exc = [jnp.exp(x - y) for x, y in zip(cum, lw)]
    p_inv = [jnp.exp(-x) for x in cum]
    p_end = [x[C - 1:C, :] for x in p_inc]
    to_end = [x * y for x, y in zip(p_inv, p_end)]

    a_st = [stack(x * p) for x, p in zip(load(na_s), p_exc)]
    r_st = [stack(x * p) for x, p in zip(load(r_s), p_inc)]
    v_bf = [stack(x).astype(BF16) for x in load(v_s)]
    xr = [cat([a, r], 0).astype(BF16) for a, r in zip(a_st, r_st)]
    b = load(b_s)
    k = load(k_s)
    y_g = [cat([stack(bb * p), stack(kk * p)], 0).astype(BF16) for bb, kk, p in zip(b, k, p_inv)]
    y_h = [cat([stack(bb * p), stack(kk * p)], 0).astype(BF16) for bb, kk, p in zip(b, k, to_end)]

    g = [_dot_nt(x, y) for x, y in zip(xr, y_g)]
    l_ab = [jnp.where(strict, x[:C2, :C2], 0.0) for x in g]
    lmk = [cat([jnp.where(strict, x[:C2, C2:], 0.0), jnp.where(incl, x[C2:, C2:], 0.0)], 0).astype(BF16)
           for x in g]
    m_rb = [jnp.where(incl, x[C2:, :C2], 0.0).astype(BF16) for x in g]

    l_bf = [x.astype(BF16) for x in l_ab]
    pw = [_dot(x, x) for x in l_bf]
    acc = [eye + x for x in l_ab]
    for _ in range(1, int(math.log2(C)) - 1):
        ps = [_dot(p.astype(BF16), cat([s, p], 1).astype(BF16)) for p, s in zip(pw, acc)]
        acc = [s + x[:, :C2] for s, x in zip(acc, ps)]
        pw = [x[:, C2:] for x in ps]
    acc = [s + _dot(p.astype(BF16), s.astype(BF16)) for p, s in zip(pw, acc)]

    lm = [_dot(x, vv) for x, vv in zip(lmk, v_bf)]
    av = [_dot(t.astype(BF16), cat([a, x[:C2]], 1).astype(BF16)).astype(BF16)
          for t, a, x in zip(acc, a_st, lm)]
    ry = [_dot(m, x) + cat([r, y[C2:]], 1) for m, x, r, y in zip(m_rb, av, r_st, lm)]
    zero = jnp.zeros((C2, C2), BF16)
    md = [_dot_tn(yh, cat([x, cat([zero, vv], 1)], 0)) for yh, x, vv in zip(y_h, av, v_bf)]
    rmp = [cat([y[:, :C2], jnp.where(diag, jnp.broadcast_to(pe, (C2, C2)), 0.0) + x[:, :C2]], 0).astype(BF16)
           for y, pe, x in zip(ry, p_end, md)]

    h = [h_ref[pr] for pr in pairs]
    for ci_, rows in enumerate(row_list):
        base = ci_ * len(pairs)
        rm = [_dot(rmp[base + pr], h[pr].astype(BF16)) for pr in pairs]
        h = [rm[pr][C2:] + md[base + pr][:, C2:] for pr in pairs]
        for pr in pairs:
            y_st = rm[pr][:C2] + ry[base + pr][:, C2:]
            y_s[rows, lanes[pr]] = y_st[:C] + y_st[C:]
    for pr in pairs:
        h_ref[pr] = h[pr]


def _rwkv_kernel(z_ref, zh_ref, mu_ref, w0_ref, a0_ref, w2a_ref, g2_ref, kk_ref, ka_ref, rk_ref,
                 gnw_ref, gnb_ref, seg_ref, tril_ref, y_ref,
                 h_ref, r_s, k_s, v_s, na_s, b_s, lw_s, cum_s, g_s, y_s):
    t = pl.program_id(1)
    TS = RWKV_TS
    D = RWKV_DIM

    @pl.when(t == 0)
    def _():
        h_ref[...] = jnp.zeros_like(h_ref)

    has_prev = jnp.where(t > 0, 1.0, 0.0).astype(F32)

    def shifted(lo, hi):
        zc = z_ref[:, lo:hi]
        prev = zh_ref[SUBLANES - 1:SUBLANES, lo:hi] * has_prev
        row = lax.broadcasted_iota(jnp.int32, zc.shape, 0)
        zp = jnp.where(row == 0, prev, pltpu.roll(zc, 1, 0))
        return zc + (zp - zc) * mu_ref[:, lo:hi]

    seg = seg_ref[...]
    r = shifted(0, D)
    k = shifted(D, 2 * D)
    v = shifted(2 * D, 3 * D)
    wa = shifted(3 * D, 3 * D + DECAY_LORA + ICLR_LORA)
    gd = shifted(3 * D + DECAY_LORA + ICLR_LORA, RWKV_COLS)

    lane = lax.broadcasted_iota(jnp.int32, wa.shape, 1)
    lora_in = jnp.where(lane < DECAY_LORA, jnp.tanh(wa), wa).astype(BF16)
    lora = _dot(lora_in, w2a_ref[...])
    lw = -math.exp(-0.5) * _sigmoid(w0_ref[...] + lora[:, :D])
    lw_s[...] = lw
    tril = tril_ref[...]
    hi = lw.astype(BF16)
    lo = (lw - hi.astype(F32)).astype(BF16)
    cum_s[...] = _dot(tril, hi) + _dot(tril, lo)
    a = _sigmoid(a0_ref[...] + lora[:, D:])
    g_s[...] = _dot(_sigmoid(gd).astype(BF16), g2_ref[...])

    kk = k * kk_ref[...]
    kk = kk * lax.rsqrt(jnp.maximum(_segsum(kk * kk, seg), 1e-24))
    k2 = k * (1.0 + (a - 1.0) * ka_ref[...])
    r_s[...] = r
    k_s[...] = k2
    v_s[...] = v
    na_s[...] = -kk
    b_s[...] = kk * a

    def chunk_body(c, carry):
        row_list = [pl.ds(pl.multiple_of((c * CHUNKS_PER_ITER + i) * CHUNK, CHUNK), CHUNK)
                    for i in range(CHUNKS_PER_ITER)]
        _scan_chunks(row_list, h_ref, r_s, k_s, v_s, na_s, b_s, lw_s, cum_s, y_s)
        return carry

    lax.fori_loop(0, TS // (CHUNK * CHUNKS_PER_ITER), chunk_body, 0)

    y = y_s[...]
    inv_n = 1.0 / HEAD_DIM
    mean = _segsum(y, seg) * inv_n
    yc = y - mean
    var = _segsum(yc * yc, seg) * inv_n
    yn = yc * lax.rsqrt(var + GN_EPS) * gnw_ref[...] + gnb_ref[...]
    bonus = _segsum(r_s[...] * k_s[...] * rk_ref[...], seg) * v_s[...]
    y_ref[...] = ((yn + bonus) * g_s[...]).astype(BF16)


def _rwkv(zr, mu, w0, a0, w2a, g2, k_k, k_a, r_k, gn_w, gn_b, seg, B, S):
    TS = RWKV_TS
    t_idx = np.arange(TS)
    tril = jnp.asarray((t_idx[:, None] // CHUNK == t_idx[None, :] // CHUNK)
                       & (t_idx[None, :] <= t_idx[:, None]), BF16)
    consts = [mu, w0, a0, w2a, g2, k_k, k_a, r_k, gn_w, gn_b, seg, tril]
    halo_blocks = TS // SUBLANES
    stage = pltpu.VMEM((TS, RWKV_DIM), F32)
    return pl.pallas_call(
        _rwkv_kernel,
        grid=(B, S // TS),
        in_specs=[pl.BlockSpec((None, TS, RWKV_COLS), lambda b, t: (b, t, 0)),
                  pl.BlockSpec((None, SUBLANES, RWKV_COLS),
                               lambda b, t: (b, jnp.maximum(t * halo_blocks - 1, 0), 0))]
                 + [_const_spec(c) for c in consts],
        out_specs=pl.BlockSpec((None, TS, RWKV_DIM), lambda b, t: (b, t, 0)),
        out_shape=jax.ShapeDtypeStruct((B, S, RWKV_DIM), BF16),
        scratch_shapes=[pltpu.VMEM((RWKV_HEADS // 2, 2 * CHUNK, 2 * CHUNK), F32)] + [stage] * 9,
        compiler_params=_params("parallel", "arbitrary"),
        name="rwkv",
    )(zr, zr, *consts)


def _attn_group(gi, dil, S, x_s, o_s, l_s):
    L = S // dil
    nb = L // BAND_BLOCK
    Q = BAND_BLOCK
    qq = lax.broadcasted_iota(jnp.int32, (Q, Q), 0)
    kq = lax.broadcasted_iota(jnp.int32, (Q, Q), 1)
    lane2 = lax.broadcasted_iota(jnp.int32, (Q, 2 * LANES), 1)
    half = HEAD_DIM // 2

    def unit(u, carry):
        if nb > 1:
            res = u // nb
            blk = u % nb
        else:
            res = u
            blk = 0
        start = res + blk * (Q * dil)

        def load(slab, st):
            return x_s[slab, pl.ds(st, Q, stride=dil), :]

        def both(slab, st):
            return jnp.concatenate([load(slab, st), load(slab + 1, st)], axis=1).astype(BF16)

        q = both(0, start)
        kcur = both(2, start)
        vcur = both(4, start)
        if nb > 1:
            pstart = res + jnp.maximum(blk - 1, 0) * (Q * dil)
            keys = jnp.concatenate([both(2, pstart), kcur], axis=0)
            vals = jnp.concatenate([both(4, pstart), vcur], axis=0)
            off = jnp.where(blk > 0, 0, 2 * Q)
            valid = jnp.concatenate([kq >= qq + off, kq <= qq], axis=1)
        else:
            keys, vals = kcur, vcur
            valid = kq <= qq

        o = jnp.zeros((Q, 2 * LANES), F32)
        lse = jnp.zeros((Q, 2 * LANES), F32)
        for j in range(HEADS_PER_GROUP):
            qmask = (lane2 % LANES) // half == j
            s = _dot_nt(jnp.where(qmask, q, jnp.zeros_like(q)), keys)
            s = jnp.where(valid, s, NEG_INF)
            m = jnp.max(s, axis=-1, keepdims=True)
            e = jnp.exp(s - m)
            l = jnp.sum(e, axis=-1, keepdims=True)
            oj = _dot(e.astype(BF16), vals) / l
            hmask = lane2 // HEAD_DIM == j
            o = jnp.where(hmask, oj, o)
            lse = jnp.where(hmask, m + jnp.log(l), lse)
        rows = pl.ds(start, Q, stride=dil)
        o_s[gi, 0, rows, :] = o[:, :LANES]
        o_s[gi, 1, rows, :] = o[:, LANES:]
        l_s[gi, 0, rows, :] = lse[:, :LANES]
        l_s[gi, 1, rows, :] = lse[:, LANES:]
        return carry

    lax.fori_loop(0, dil * nb, unit, 0)


def _attn_kernel(za_ref, pos_ref, invf_ref, qga_ref, qgb_ref, kga_ref, kgb_ref, seg_ref, out_ref,
                 x_s, o_s, l_s, cs_s):
    g = pl.program_id(1)
    S = za_ref.shape[0]
    R = ATTN_ROWS

    @pl.when(g == 0)
    def _():
        def body(i, carry):
            rows = pl.ds(pl.multiple_of(i * R, R), R)
            ang = pos_ref[rows, :].astype(F32) * invf_ref[...]
            cs_s[0, rows, :] = jnp.cos(ang)
            cs_s[1, rows, :] = jnp.sin(ang)
            return carry
        lax.fori_loop(0, S // R, body, 0)

    seg = seg_ref[...]

    def prep(i, carry):
        rows = pl.ds(pl.multiple_of(i * R, R), R)
        cos = cs_s[0, rows, :]
        sin = cs_s[1, rows, :]
        for col, slab, ga_ref, gb_ref, scale in ((0, 0, qga_ref, qgb_ref, HEAD_DIM ** -0.5),
                                                 (2 * LANES, 2, kga_ref, kgb_ref, 1.0)):
            xa = za_ref[rows, col:col + LANES].astype(F32)
            xb = za_ref[rows, col + LANES:col + 2 * LANES].astype(F32)
            ms = _segsum(xa * xa + xb * xb, seg) * (1.0 / HEAD_DIM)
            inv = lax.rsqrt(ms + RMS_EPS)
            xa = xa * inv * ga_ref[...]
            xb = xb * inv * gb_ref[...]
            x_s[slab, rows, :] = (xa * cos - xb * sin) * scale
            x_s[slab + 1, rows, :] = (xb * cos + xa * sin) * scale
        x_s[4, rows, :] = za_ref[rows, 4 * LANES:5 * LANES].astype(F32)
        x_s[5, rows, :] = za_ref[rows, 5 * LANES:6 * LANES].astype(F32)
        return carry

    lax.fori_loop(0, S // R, prep, 0)

    for gi, (_, dil) in enumerate(ATTN_GROUPS):
        pl.when(g == gi)(functools.partial(_attn_group, gi, dil, S, x_s, o_s, l_s))

    @pl.when(g == len(ATTN_GROUPS) - 1)
    def _():
        def body(i, carry):
            rows = pl.ds(pl.multiple_of(i * R, R), R)
            for slab in range(2):
                ls = [l_s[gi, slab, rows, :] for gi in range(len(ATTN_GROUPS))]
                m = jnp.maximum(jnp.maximum(ls[0], ls[1]), ls[2])
                ws = [jnp.exp(x - m) for x in ls]
                num = sum(w * o_s[gi, slab, rows, :] for gi, w in enumerate(ws))
                out_ref[rows, slab * LANES:(slab + 1) * LANES] = (num / (ws[0] + ws[1] + ws[2])).astype(BF16)
            return carry
        lax.fori_loop(0, S // R, body, 0)


def _attn(za, pos, invf, qga, qgb, kga, kgb, seg32, B, S):
    consts = [invf, qga, qgb, kga, kgb, seg32]
    ng = len(ATTN_GROUPS)
    return pl.pallas_call(
        _attn_kernel,
        grid=(B, ng),
        in_specs=[pl.BlockSpec((None, S, GROUP_COLS), lambda b, g: (b, 0, g)),
                  pl.BlockSpec((None, S, 1), lambda b, g: (b, 0, 0))]
                 + [_const_spec(c) for c in consts],
        out_specs=pl.BlockSpec((None, S, ATTN_OUT_DIM), lambda b, g: (b, 0, 0)),
        out_shape=jax.ShapeDtypeStruct((B, S, ATTN_OUT_DIM), BF16),
        scratch_shapes=[pltpu.VMEM((6, S, LANES), F32), pltpu.VMEM((ng, 2, S, LANES), F32),
                        pltpu.VMEM((ng, 2, S, LANES), F32), pltpu.VMEM((2, S, LANES), F32)],
        compiler_params=_params("parallel", "arbitrary"),
        name="attn",
    )(za, pos, *consts)


def _out_kernel(x1_ref, gt_ref, yr_ref, ya_ref, p_ref, wbr_ref, wba_ref, wo_ref, n2_ref, wg_ref, wu_ref,
                wd_ref, pn_ref, pwg_ref, pwp_ref, out_ref):
    br = _dot(yr_ref[...], wbr_ref[...])
    ba = _dot(ya_ref[...], wba_ref[...])
    merged = gt_ref[:, :D_MODEL].astype(F32) * br + gt_ref[:, D_MODEL:].astype(F32) * ba
    x2 = x1_ref[...] + _dot(merged.astype(BF16), wo_ref[...])
    h = _rms(x2, n2_ref[...]).astype(BF16)
    g = _dot(h, wg_ref[...])
    u = _dot(h, wu_ref[...])
    act = (g * _sigmoid(g) * u).astype(BF16)
    x3 = x2 + 0.5 * _dot(act, wd_ref[...])
    hp = _rms(x3, pn_ref[...]).astype(BF16)
    gate = _sigmoid(_dot(hp, pwg_ref[...]))
    out_ref[...] = x3 + gate * _dot(p_ref[...].astype(BF16), pwp_ref[...])


def _out(x1, gt, yr, ya, p2d, consts):
    T = x1.shape[0]
    TM = OUT_TM
    row = lambda i: (i, 0)
    return pl.pallas_call(
        _out_kernel,
        grid=(T // TM,),
        in_specs=[pl.BlockSpec((TM, D_MODEL), row), pl.BlockSpec((TM, GATE_COLS), row),
                  pl.BlockSpec((TM, RWKV_DIM), row), pl.BlockSpec((TM, ATTN_OUT_DIM), row),
                  pl.BlockSpec((TM, PLE_DIM), row)] + [_const_spec(c) for c in consts],
        out_specs=pl.BlockSpec((TM, D_MODEL), row),
        out_shape=jax.ShapeDtypeStruct((T, D_MODEL), F32),
        compiler_params=_params("parallel"),
        name="merge_ffn2",
    )(x1, gt, yr, ya, p2d, *consts)


def _attn_column_order():
    half = HEAD_DIM // 2
    order = []
    for g in range(len(ATTN_GROUPS)):
        heads = [g * HEADS_PER_GROUP + j for j in range(HEADS_PER_GROUP)]
        for base in (0, ATTN_DIM):
            for lo in (0, half):
                for h in heads:
                    order.extend(base + h * HEAD_DIM + lo + d for d in range(half))
        for h in heads:
            order.extend(2 * ATTN_DIM + h * HEAD_DIM + d for d in range(HEAD_DIM))
    return np.asarray(order, np.int32)


def _block_ones(n, blk):
    idx = np.arange(n) // blk
    return jnp.asarray(idx[:, None] == idx[None, :], BF16)


def kernel(x, p, positions, ffn1_norm, ffn1_w_gate, ffn1_w_up, ffn1_w_down, mix_norm, w_in, rwkv_mu, rwkv_w0, rwkv_w2, rwkv_a0, rwkv_a2, rwkv_g2, rwkv_k_k, rwkv_k_a, rwkv_r_k, rwkv_gn_w, rwkv_gn_b, q_norm, k_norm, w_br_rwkv, w_br_attn, w_out, ffn2_norm, ffn2_w_gate, ffn2_w_up, ffn2_w_down, ple_norm, ple_w_gate, ple_w_proj):
    B, S, D = x.shape
    depth = p.shape[0]
    T = B * S
    half = HEAD_DIM // 2
    bf = lambda w: w.astype(BF16)
    rowvec = lambda a: a.reshape(1, -1).astype(F32)

    inv_freq = 1.0 / (ROPE_THETA ** (jnp.arange(0, HEAD_DIM, 2, dtype=F32) / HEAD_DIM))
    invf = jnp.tile(inv_freq, HEADS_PER_GROUP).reshape(1, LANES)
    pos = positions.reshape(B, S, 1)
    seg64 = _block_ones(2 * LANES, HEAD_DIM)
    seg32 = _block_ones(LANES, half)
    attn_cols = RWKV_COLS + _attn_column_order()

    xc = x.reshape(T, D)
    for i in range(depth):
        w_in_i = w_in[i]
        w_in_perm = jnp.concatenate(
            [w_in_i[:, :RWKV_COLS], w_in_i[:, attn_cols], w_in_i[:, RWKV_COLS + ATTN_COLS:]], axis=1)
        zeros = jnp.zeros((DECAY_LORA, RWKV_DIM), F32)
        w2a = jnp.concatenate([jnp.concatenate([rwkv_w2[i], zeros], axis=1),
                               jnp.concatenate([zeros, rwkv_a2[i]], axis=1)], axis=0)

        x1, h2 = _ffn1(xc, rowvec(ffn1_norm[i]), bf(ffn1_w_gate[i]), bf(ffn1_w_up[i]), bf(ffn1_w_down[i]),
                       rowvec(mix_norm[i]))
        zr, za, gt = _inproj(h2, bf(w_in_perm))
        yr = _rwkv(zr.reshape(B, S, RWKV_COLS), rowvec(rwkv_mu[i]), rowvec(rwkv_w0[i]), rowvec(rwkv_a0[i]),
                   bf(w2a), bf(rwkv_g2[i]), rowvec(rwkv_k_k[i]), rowvec(rwkv_k_a[i]), rowvec(rwkv_r_k[i]),
                   rowvec(rwkv_gn_w[i]), rowvec(rwkv_gn_b[i]), seg64, B, S)
        tile4 = lambda a: jnp.tile(a, HEADS_PER_GROUP).reshape(1, LANES).astype(F32)
        ya = _attn(za.reshape(B, S, ATTN_COLS), pos, invf, tile4(q_norm[i][:half]), tile4(q_norm[i][half:]),
                   tile4(k_norm[i][:half]), tile4(k_norm[i][half:]), seg32, B, S)
        consts = [bf(w_br_rwkv[i]), bf(w_br_attn[i]), bf(w_out[i]), rowvec(ffn2_norm[i]), bf(ffn2_w_gate[i]),
                  bf(ffn2_w_up[i]), bf(ffn2_w_down[i]), rowvec(ple_norm[i]), bf(ple_w_gate[i]),
                  bf(ple_w_proj[i])]
        xc = _out(x1, gt, yr.reshape(T, RWKV_DIM), ya.reshape(T, ATTN_OUT_DIM), p[i].reshape(T, PLE_DIM), consts)
    return xc.reshape(B, S, D)
```

```python
import functools
import math

import numpy as np
import jax
import jax.numpy as jnp
from jax import lax
from jax.experimental import pallas as pl
from jax.experimental.pallas import tpu as pltpu

F32 = jnp.float32
BF16 = jnp.bfloat16

D_MODEL = 1024
PLE_DIM = 256
HEAD_DIM = 64
RWKV_HEADS = 8
RWKV_DIM = RWKV_HEADS * HEAD_DIM
DECAY_LORA = 64
ICLR_LORA = 64
GATE_LORA = 128
GN_EPS = 64e-5
ATTN_GROUPS = ((128, 1), (512, 4), (2048, 16))
HEADS_PER_GROUP = 4
ATTN_DIM = HEADS_PER_GROUP * len(ATTN_GROUPS) * HEAD_DIM
ATTN_OUT_DIM = HEADS_PER_GROUP * HEAD_DIM
BAND_BLOCK = 128
ROPE_THETA = 10000.0
NEG_INF = -1e30
D_FF = 2816
RMS_EPS = 1e-6
LOG2E = math.log2(math.e)
LN2 = math.log(2.0)
RWKV_COLS = 3 * RWKV_DIM + DECAY_LORA + ICLR_LORA + GATE_LORA
ATTN_COLS = 3 * ATTN_DIM
GATE_COLS = 2 * D_MODEL

LANES = 128
SUBLANES = 8
VMEM_LIMIT_BYTES = 60 * 1024 * 1024

FFN_TM = 512
OUT_TM = 256
RWKV_TS = 256
CHUNK = 64
CHUNKS_PER_ITER = 4
ATTN_ROWS = 256
ATTN_UNITS = 4


def _dot(a, b):
    return jnp.dot(a, b, preferred_element_type=F32)


def _dot_nt(a, b):
    return lax.dot_general(a, b, (((1,), (1,)), ((), ())), preferred_element_type=F32)


def _dot_tn(a, b):
    return lax.dot_general(a, b, (((0,), (0,)), ((), ())), preferred_element_type=F32)


def _rms(x, gain):
    return x * lax.rsqrt(jnp.mean(x * x, axis=-1, keepdims=True) + RMS_EPS) * gain


def _sigmoid(x):
    return 1.0 / (1.0 + jnp.exp(-x))


def _segsum(x, seg):
    w = seg.shape[0]
    parts = [_dot(x[:, lo:lo + w].astype(BF16), seg) for lo in range(0, x.shape[1], w)]
    return parts[0] if len(parts) == 1 else jnp.concatenate(parts, axis=1)


def _const_spec(arr):
    nd = arr.ndim
    return pl.BlockSpec(arr.shape, lambda *_: (0,) * nd, pipeline_mode=pl.Buffered(1))


def _params(*sem):
    return pltpu.CompilerParams(dimension_semantics=sem, vmem_limit_bytes=VMEM_LIMIT_BYTES)


def _ffn1_kernel(x_ref, n1_ref, wg_ref, wu_ref, wd_ref, n2_ref, x1_ref, h2_ref):
    x = x_ref[...]
    h = _rms(x, n1_ref[...]).astype(BF16)
    g = _dot(h, wg_ref[...])
    u = _dot(h, wu_ref[...])
    act = (g * _sigmoid(g) * u).astype(BF16)
    x1 = x + 0.5 * _dot(act, wd_ref[...])
    x1_ref[...] = x1
    h2_ref[...] = _rms(x1, n2_ref[...]).astype(BF16)


def _ffn1(x2d, n1, wg, wu, wd, n2):
    T = x2d.shape[0]
    row = lambda i: (i, 0)
    return pl.pallas_call(
        _ffn1_kernel,
        grid=(T // FFN_TM,),
        in_specs=[pl.BlockSpec((FFN_TM, D_MODEL), row), _const_spec(n1), _const_spec(wg),
                  _const_spec(wu), _const_spec(wd), _const_spec(n2)],
        out_specs=[pl.BlockSpec((FFN_TM, D_MODEL), row), pl.BlockSpec((FFN_TM, D_MODEL), row)],
        out_shape=[jax.ShapeDtypeStruct((T, D_MODEL), F32), jax.ShapeDtypeStruct((T, D_MODEL), BF16)],
        compiler_params=_params("parallel"),
        name="ffn1",
    )(x2d, n1, wg, wu, wd, n2)


def _inproj_kernel(h_ref, w_ref, zr_ref, za_ref, gt_ref):
    h = h_ref[...]
    zr_ref[...] = _dot(h, w_ref[:, :RWKV_COLS])
    za_ref[...] = _dot(h, w_ref[:, RWKV_COLS:RWKV_COLS + ATTN_COLS]).astype(BF16)
    gt_ref[...] = _sigmoid(_dot(h, w_ref[:, RWKV_COLS + ATTN_COLS:])).astype(BF16)


def _inproj(h2, w):
    T = h2.shape[0]
    row = lambda i: (i, 0)
    return pl.pallas_call(
        _inproj_kernel,
        grid=(T // FFN_TM,),
        in_specs=[pl.BlockSpec((FFN_TM, D_MODEL), row), _const_spec(w)],
        out_specs=[pl.BlockSpec((FFN_TM, RWKV_COLS), row), pl.BlockSpec((FFN_TM, ATTN_COLS), row),
                   pl.BlockSpec((FFN_TM, GATE_COLS), row)],
        out_shape=[jax.ShapeDtypeStruct((T, RWKV_COLS), F32), jax.ShapeDtypeStruct((T, ATTN_COLS), BF16),
                   jax.ShapeDtypeStruct((T, GATE_COLS), BF16)],
        compiler_params=_params("parallel"),
        name="inproj",
    )(h2, w)


def _scan_chunks(row_list, h_ref, r_s, k_s, v_s, na_s, b_s, lw_s, cum_s, y_s):
    C = CHUNK
    C2 = 2 * C
    pairs = range(RWKV_HEADS // 2)
    rows_of = [rows for rows in row_list for _ in pairs]
    lanes = [slice(pr * LANES, (pr + 1) * LANES) for _ in row_list for pr in pairs]
    first = lax.broadcasted_iota(jnp.int32, (C, LANES), 1) < HEAD_DIM
    ri = lax.broadcasted_iota(jnp.int32, (C2, C2), 0)
    ci = lax.broadcasted_iota(jnp.int32, (C2, C2), 1)
    strict = ci < ri
    incl = ci <= ri
    diag = ci == ri
    eye = jnp.where(diag, 1.0, 0.0)

    def stack(x):
        return jnp.concatenate([jnp.where(first, x, 0.0), jnp.where(first, 0.0, x)], axis=0)

    def cat(parts, axis):
        return jnp.concatenate(parts, axis=axis)

    def load(ref):
        return [ref[rows, ln] for rows, ln in zip(rows_of, lanes)]

    lw = load(lw_s)
    cum = load(cum_s)
    p_inc = [jnp.exp(x) for x in cum]
    p_exc = [jnp.exp(x - y) for x, y in zip(cum, lw)]
    p_inv = [jnp.exp(-x) for x in cum]
    p_end = [x[C - 1:C, :] for x in p_inc]
    to_end = [x * y for x, y in zip(p_inv, p_end)]

    a_st = [stack(x * p) for x, p in zip(load(na_s), p_exc)]
    r_st = [stack(x * p) for x, p in zip(load(r_s), p_inc)]
    v_bf = [stack(x).astype(BF16) for x in load(v_s)]
    xr = [cat([a, r], 0).astype(BF16) for a, r in zip(a_st, r_st)]
    b = load(b_s)
    k = load(k_s)
    y_g = [cat([stack(bb * p), stack(kk * p)], 0).astype(BF16) for bb, kk, p in zip(b, k, p_inv)]
    y_h = [cat([stack(bb * p), stack(kk * p)], 0).astype(BF16) for bb, kk, p in zip(b, k, to_end)]

    g = [_dot_nt(x, y) for x, y in zip(xr, y_g)]
    l_ab = [jnp.where(strict, x[:C2, :C2], 0.0) for x in g]
    lmk = [cat([jnp.where(strict, x[:C2, C2:], 0.0), jnp.where(incl, x[C2:, C2:], 0.0)], 0).astype(BF16)
           for x in g]
    m_rb = [jnp.where(incl, x[C2:, :C2], 0.0).astype(BF16) for x in g]

    l_bf = [x.astype(BF16) for x in l_ab]
    pw = [_dot(x, x) for x in l_bf]
    acc = [eye + x for x in l_ab]
    for _ in range(1, int(math.log2(C)) - 1):
        ps = [_dot(p.astype(BF16), cat([s, p], 1).astype(BF16)) for p, s in zip(pw, acc)]
        acc = [s + x[:, :C2] for s, x in zip(acc, ps)]
        pw = [x[:, C2:] for x in ps]
    acc = [s + _dot(p.astype(BF16), s.astype(BF16)) for p, s in zip(pw, acc)]

    lm = [_dot(x, vv) for x, vv in zip(lmk, v_bf)]
    av = [_dot(t.astype(BF16), cat([a, x[:C2]], 1).astype(BF16)).astype(BF16)
          for t, a, x in zip(acc, a_st, lm)]
    ry = [_dot(m, x) + cat([r, y[C2:]], 1) for m, x, r, y in zip(m_rb, av, r_st, lm)]
    zero = jnp.zeros((C2, C2), BF16)
    md = [_dot_tn(yh, cat([x, cat([zero, vv], 1)], 0)) for yh, x, vv in zip(y_h, av, v_bf)]
    rmp = [cat([y[:, :C2], jnp.where(diag, jnp.broadcast_to(pe, (C2, C2)), 0.0) + x[:, :C2]], 0).astype(BF16)
           for y, pe, x in zip(ry, p_end, md)]

    h = [h_ref[pr] for pr in pairs]
    for ci_, rows in enumerate(row_list):
        base = ci_ * len(pairs)
        rm = [_dot(rmp[base + pr], h[pr].astype(BF16)) for pr in pairs]
        h = [rm[pr][C2:] + md[base + pr][:, C2:] for pr in pairs]
        for pr in pairs:
            y_st = rm[pr][:C2] + ry[base + pr][:, C2:]
            y_s[rows, lanes[pr]] = y_st[:C] + y_st[C:]
    for pr in pairs:
        h_ref[pr] = h[pr]


def _rwkv_kernel(z_ref, zh_ref, mu_ref, w0_ref, a0_ref, w2a_ref, g2_ref, kk_ref, ka_ref, rk_ref,
                 gnw_ref, gnb_ref, seg_ref, tril_ref, y_ref,
                 h_ref, r_s, k_s, v_s, na_s, b_s, lw_s, cum_s, g_s, y_s):
    t = pl.program_id(1)
    TS = RWKV_TS
    D = RWKV_DIM

    @pl.when(t == 0)
    def _():
        h_ref[...] = jnp.zeros_like(h_ref)

    has_prev = jnp.where(t > 0, 1.0, 0.0).astype(F32)

    def shifted(lo, hi):
        zc = z_ref[:, lo:hi]
        prev = zh_ref[SUBLANES - 1:SUBLANES, lo:hi] * has_prev
        row = lax.broadcasted_iota(jnp.int32, zc.shape, 0)
        zp = jnp.where(row == 0, prev, pltpu.roll(zc, 1, 0))
        return zc + (zp - zc) * mu_ref[:, lo:hi]

    seg = seg_ref[...]
    r = shifted(0, D)
    k = shifted(D, 2 * D)
    v = shifted(2 * D, 3 * D)
    wa = shifted(3 * D, 3 * D + DECAY_LORA + ICLR_LORA)
    gd = shifted(3 * D + DECAY_LORA + ICLR_LORA, RWKV_COLS)

    lane = lax.broadcasted_iota(jnp.int32, wa.shape, 1)
    lora_in = jnp.where(lane < DECAY_LORA, jnp.tanh(wa), wa).astype(BF16)
    lora = _dot(lora_in, w2a_ref[...])
    lw = -math.exp(-0.5) * _sigmoid(w0_ref[...] + lora[:, :D])
    lw_s[...] = lw
    tril = tril_ref[...]
    hi = lw.astype(BF16)
    lo = (lw - hi.astype(F32)).astype(BF16)
    cum_s[...] = _dot(tril, hi) + _dot(tril, lo)
    a = _sigmoid(a0_ref[...] + lora[:, D:])
    g_s[...] = _dot(_sigmoid(gd).astype(BF16), g2_ref[...])

    kk = k * kk_ref[...]
    kk = kk * lax.rsqrt(jnp.maximum(_segsum(kk * kk, seg), 1e-24))
    k2 = k * (1.0 + (a - 1.0) * ka_ref[...])
    r_s[...] = r
    k_s[...] = k2
    v_s[...] = v
    na_s[...] = -kk
    b_s[...] = kk * a

    def chunk_body(c, carry):
        row_list = [pl.ds(pl.multiple_of((c * CHUNKS_PER_ITER + i) * CHUNK, CHUNK), CHUNK)
                    for i in range(CHUNKS_PER_ITER)]
        _scan_chunks(row_list, h_ref, r_s, k_s, v_s, na_s, b_s, lw_s, cum_s, y_s)
        return carry

    lax.fori_loop(0, TS // (CHUNK * CHUNKS_PER_ITER), chunk_body, 0)

    y = y_s[...]
    inv_n = 1.0 / HEAD_DIM
    mean = _segsum(y, seg) * inv_n
    yc = y - mean
    var = _segsum(yc * yc, seg) * inv_n
    yn = yc * lax.rsqrt(var + GN_EPS) * gnw_ref[...] + gnb_ref[...]
    bonus = _segsum(r_s[...] * k_s[...] * rk_ref[...], seg) * v_s[...]
    y_ref[...] = ((yn + bonus) * g_s[...]).astype(BF16)


def _rwkv(zr, mu, w0, a0, w2a, g2, k_k, k_a, r_k, gn_w, gn_b, seg, B, S):
    TS = RWKV_TS
    t_idx = np.arange(TS)
    tril = jnp.asarray((t_idx[:, None] // CHUNK == t_idx[None, :] // CHUNK)
                       & (t_idx[None, :] <= t_idx[:, None]), BF16)
    consts = [mu, w0, a0, w2a, g2, k_k, k_a, r_k, gn_w, gn_b, seg, tril]
    halo_blocks = TS // SUBLANES
    stage = pltpu.VMEM((TS, RWKV_DIM), F32)
    return pl.pallas_call(
        _rwkv_kernel,
        grid=(B, S // TS),
        in_specs=[pl.BlockSpec((None, TS, RWKV_COLS), lambda b, t: (b, t, 0)),
                  pl.BlockSpec((None, SUBLANES, RWKV_COLS),
                               lambda b, t: (b, jnp.maximum(t * halo_blocks - 1, 0), 0))]
                 + [_const_spec(c) for c in consts],
        out_specs=pl.BlockSpec((None, TS, RWKV_DIM), lambda b, t: (b, t, 0)),
        out_shape=jax.ShapeDtypeStruct((B, S, RWKV_DIM), BF16),
        scratch_shapes=[pltpu.VMEM((RWKV_HEADS // 2, 2 * CHUNK, 2 * CHUNK), F32)] + [stage] * 9,
        compiler_params=_params("parallel", "arbitrary"),
        name="rwkv",
    )(zr, zr, *consts)


def _attn_group(gi, dil, S, x_s, o_s, l_s):
    L = S // dil
    nb = L // BAND_BLOCK
    Q = BAND_BLOCK
    qq = lax.broadcasted_iota(jnp.int32, (Q, Q), 0)
    kq = lax.broadcasted_iota(jnp.int32, (Q, Q), 1)
    lane2 = lax.broadcasted_iota(jnp.int32, (Q, 2 * LANES), 1)

    heads = range(HEADS_PER_GROUP)
    hmask = [lane2 // HEAD_DIM == j for j in heads]

    def load(slab, st):
        return jnp.concatenate([x_s[slab, pl.ds(st, Q, stride=dil), :],
                                x_s[slab + 1, pl.ds(st, Q, stride=dil), :]], axis=1).astype(BF16)

    def step(it, carry):
        starts, qs, keys, vals, valid = [], [], [], [], []
        for n in range(ATTN_UNITS):
            u = it * ATTN_UNITS + n
            res, blk = (u // nb, u % nb) if nb > 1 else (u, 0)
            start = res + blk * (Q * dil)
            starts.append(start)
            qs.append(load(0, start))
            kcur = load(2, start)
            vcur = load(4, start)
            if nb > 1:
                pstart = res + jnp.maximum(blk - 1, 0) * (Q * dil)
                keys.append(jnp.concatenate([load(2, pstart), kcur], axis=0))
                vals.append(jnp.concatenate([load(4, pstart), vcur], axis=0))
                off = jnp.where(blk > 0, 0, 2 * Q)
                valid.append(jnp.concatenate([kq >= qq + off, kq <= qq], axis=1))
            else:
                keys.append(kcur)
                vals.append(vcur)
                valid.append(kq <= qq)
        chains = [(n, j) for n in range(ATTN_UNITS) for j in heads]
        s = [_dot_nt(jnp.where(hmask[j], qs[n], jnp.zeros_like(qs[n])), keys[n]) for n, j in chains]
        s = [jnp.where(valid[n], x, NEG_INF) for (n, j), x in zip(chains, s)]
        m = [jnp.max(x, axis=-1, keepdims=True) for x in s]
        e = [jnp.exp2(x - y) for x, y in zip(s, m)]
        l = [jnp.sum(x, axis=-1, keepdims=True) for x in e]
        pv = [_dot(x.astype(BF16), vals[n]) for (n, j), x in zip(chains, e)]
        for n in range(ATTN_UNITS):
            o = jnp.zeros((Q, 2 * LANES), F32)
            lse = jnp.zeros((Q, 2 * LANES), F32)
            for j in heads:
                c = n * HEADS_PER_GROUP + j
                o = jnp.where(hmask[j], pv[c] / l[c], o)
                lse = jnp.where(hmask[j], m[c] * LN2 + jnp.log(l[c]), lse)
            rows = pl.ds(starts[n], Q, stride=dil)
            o_s[gi, 0, rows, :] = o[:, :LANES]
            o_s[gi, 1, rows, :] = o[:, LANES:]
            l_s[gi, 0, rows, :] = lse[:, :LANES]
            l_s[gi, 1, rows, :] = lse[:, LANES:]
        return carry

    lax.fori_loop(0, dil * nb // ATTN_UNITS, step, 0)


def _attn_kernel(q_ref, k_ref, v_ref, pos_ref, invf_ref, qg_ref, qgp_ref, kg_ref, kgp_ref, seg_ref, rot_ref,
                 out_ref, x_s, o_s, l_s, cs_s):
    g = pl.program_id(1)
    S = q_ref.shape[0]
    R = ATTN_ROWS

    @pl.when(g == 0)
    def _():
        def body(i, carry):
            rows = pl.ds(pl.multiple_of(i * R, R), R)
            ang = pos_ref[rows, :].astype(F32) * invf_ref[...]
            cs_s[0, rows, :] = jnp.cos(ang)
            cs_s[1, rows, :] = jnp.sin(ang)
            return carry
        lax.fori_loop(0, S // R, body, 0)

    seg = seg_ref[...]
    rot = rot_ref[...]

    def prep(i, carry):
        rows = pl.ds(pl.multiple_of(i * R, R), R)
        cos = cs_s[0, rows, :]
        sin = cs_s[1, rows, :]
        for src, slab, gain_ref, gainp_ref, scale in ((q_ref, 0, qg_ref, qgp_ref, HEAD_DIM ** -0.5 * LOG2E),
                                                      (k_ref, 2, kg_ref, kgp_ref, 1.0)):
            for hf in range(2):
                x = src[rows, hf * LANES:(hf + 1) * LANES]
                xf = x.astype(F32)
                inv = lax.rsqrt(_segsum(xf * xf, seg) * (1.0 / HEAD_DIM) + RMS_EPS) * scale
                xr = _dot(x, rot)
                x_s[slab + hf, rows, :] = (xf * gain_ref[...] * cos + xr * gainp_ref[...] * sin) * inv
        for hf in range(2):
            x_s[4 + hf, rows, :] = v_ref[rows, hf * LANES:(hf + 1) * LANES].astype(F32)
        return carry

    lax.fori_loop(0, S // R, prep, 0)

    for gi, (_, dil) in enumerate(ATTN_GROUPS):
        pl.when(g == gi)(functools.partial(_attn_group, gi, dil, S, x_s, o_s, l_s))

    @pl.when(g == len(ATTN_GROUPS) - 1)
    def _():
        def body(i, carry):
            rows = pl.ds(pl.multiple_of(i * R, R), R)
            for slab in range(2):
                ls = [l_s[gi, slab, rows, :] for gi in range(len(ATTN_GROUPS))]
                m = jnp.maximum(jnp.maximum(ls[0], ls[1]), ls[2])
                ws = [jnp.exp(x - m) for x in ls]
                num = sum(w * o_s[gi, slab, rows, :] for gi, w in enumerate(ws))
                out_ref[rows, slab * LANES:(slab + 1) * LANES] = (num / (ws[0] + ws[1] + ws[2])).astype(BF16)
            return carry
        lax.fori_loop(0, S // R, body, 0)


def _attn(za, pos, consts, B, S):
    ng = len(ATTN_GROUPS)
    gw = ATTN_OUT_DIM
    return pl.pallas_call(
        _attn_kernel,
        grid=(B, ng),
        in_specs=[pl.BlockSpec((None, S, gw), lambda b, g: (b, 0, g)),
                  pl.BlockSpec((None, S, gw), lambda b, g: (b, 0, ng + g)),
                  pl.BlockSpec((None, S, gw), lambda b, g: (b, 0, 2 * ng + g)),
                  pl.BlockSpec((None, S, 1), lambda b, g: (b, 0, 0))]
                 + [_const_spec(c) for c in consts],
        out_specs=pl.BlockSpec((None, S, ATTN_OUT_DIM), lambda b, g: (b, 0, 0)),
        out_shape=jax.ShapeDtypeStruct((B, S, ATTN_OUT_DIM), BF16),
        scratch_shapes=[pltpu.VMEM((6, S, LANES), F32), pltpu.VMEM((ng, 2, S, LANES), F32),
                        pltpu.VMEM((ng, 2, S, LANES), F32), pltpu.VMEM((2, S, LANES), F32)],
        compiler_params=_params("parallel", "arbitrary"),
        name="attn",
    )(za, za, za, pos, *consts)


def _out_kernel(x1_ref, gt_ref, yr_ref, ya_ref, p_ref, wbr_ref, wba_ref, wo_ref, n2_ref, wg_ref, wu_ref,
                wd_ref, pn_ref, pwg_ref, pwp_ref, out_ref):
    br = _dot(yr_ref[...], wbr_ref[...])
    ba = _dot(ya_ref[...], wba_ref[...])
    merged = gt_ref[:, :D_MODEL].astype(F32) * br + gt_ref[:, D_MODEL:].astype(F32) * ba
    x2 = x1_ref[...] + _dot(merged.astype(BF16), wo_ref[...])
    h = _rms(x2, n2_ref[...]).astype(BF16)
    g = _dot(h, wg_ref[...])
    u = _dot(h, wu_ref[...])
    act = (g * _sigmoid(g) * u).astype(BF16)
    x3 = x2 + 0.5 * _dot(act, wd_ref[...])
    hp = _rms(x3, pn_ref[...]).astype(BF16)
    gate = _sigmoid(_dot(hp, pwg_ref[...]))
    out_ref[...] = x3 + gate * _dot(p_ref[...].astype(BF16), pwp_ref[...])


def _out(x1, gt, yr, ya, p2d, consts):
    T = x1.shape[0]
    TM = OUT_TM
    row = lambda i: (i, 0)
    return pl.pallas_call(
        _out_kernel,
        grid=(T // TM,),
        in_specs=[pl.BlockSpec((TM, D_MODEL), row), pl.BlockSpec((TM, GATE_COLS), row),
                  pl.BlockSpec((TM, RWKV_DIM), row), pl.BlockSpec((TM, ATTN_OUT_DIM), row),
                  pl.BlockSpec((TM, PLE_DIM), row)] + [_const_spec(c) for c in consts],
        out_specs=pl.BlockSpec((TM, D_MODEL), row),
        out_shape=jax.ShapeDtypeStruct((T, D_MODEL), F32),
        compiler_params=_params("parallel"),
        name="merge_ffn2",
    )(x1, gt, yr, ya, p2d, *consts)


def _rotate_half_matrix():
    half = HEAD_DIM // 2
    m = np.zeros((LANES, LANES), np.float32)
    for lane in range(LANES):
        if lane % HEAD_DIM < half:
            m[lane + half, lane] = -1.0
        else:
            m[lane - half, lane] = 1.0
    return jnp.asarray(m, BF16)


def _block_ones(n, blk):
    idx = np.arange(n) // blk
    return jnp.asarray(idx[:, None] == idx[None, :], BF16)


def kernel(x, p, positions, ffn1_norm, ffn1_w_gate, ffn1_w_up, ffn1_w_down, mix_norm, w_in, rwkv_mu, rwkv_w0, rwkv_w2, rwkv_a0, rwkv_a2, rwkv_g2, rwkv_k_k, rwkv_k_a, rwkv_r_k, rwkv_gn_w, rwkv_gn_b, q_norm, k_norm, w_br_rwkv, w_br_attn, w_out, ffn2_norm, ffn2_w_gate, ffn2_w_up, ffn2_w_down, ple_norm, ple_w_gate, ple_w_proj):
    B, S, D = x.shape
    depth = p.shape[0]
    T = B * S
    half = HEAD_DIM // 2
    bf = lambda w: w.astype(BF16)
    rowvec = lambda a: a.reshape(1, -1).astype(F32)

    inv_freq = 1.0 / (ROPE_THETA ** (jnp.arange(0, HEAD_DIM, 2, dtype=F32) / HEAD_DIM))
    invf = jnp.tile(inv_freq, HEADS_PER_GROUP).reshape(1, LANES)
    pos = positions.reshape(B, S, 1)
    seg64 = _block_ones(2 * LANES, HEAD_DIM)
    seg_head = _block_ones(LANES, HEAD_DIM)
    rot = _rotate_half_matrix()
    heads_per_slab = LANES // HEAD_DIM
    gain = lambda a: jnp.tile(a, heads_per_slab).reshape(1, LANES).astype(F32)
    partner = lambda a: gain(jnp.concatenate([a[half:], a[:half]]))

    xc = x.reshape(T, D)
    for i in range(depth):
        zeros = jnp.zeros((DECAY_LORA, RWKV_DIM), F32)
        w2a = jnp.concatenate([jnp.concatenate([rwkv_w2[i], zeros], axis=1),
                               jnp.concatenate([zeros, rwkv_a2[i]], axis=1)], axis=0)

        x1, h2 = _ffn1(xc, rowvec(ffn1_norm[i]), bf(ffn1_w_gate[i]), bf(ffn1_w_up[i]), bf(ffn1_w_down[i]),
                       rowvec(mix_norm[i]))
        zr, za, gt = _inproj(h2, bf(w_in[i]))
        yr = _rwkv(zr.reshape(B, S, RWKV_COLS), rowvec(rwkv_mu[i]), rowvec(rwkv_w0[i]), rowvec(rwkv_a0[i]),
                   bf(w2a), bf(rwkv_g2[i]), rowvec(rwkv_k_k[i]), rowvec(rwkv_k_a[i]), rowvec(rwkv_r_k[i]),
                   rowvec(rwkv_gn_w[i]), rowvec(rwkv_gn_b[i]), seg64, B, S)
        attn_consts = [invf, gain(q_norm[i]), partner(q_norm[i]), gain(k_norm[i]), partner(k_norm[i]),
                       seg_head, rot]
        ya = _attn(za.reshape(B, S, ATTN_COLS), pos, attn_consts, B, S)
        consts = [bf(w_br_rwkv[i]), bf(w_br_attn[i]), bf(w_out[i]), rowvec(ffn2_norm[i]), bf(ffn2_w_gate[i]),
                  bf(ffn2_w_up[i]), bf(ffn2_w_down[i]), rowvec(ple_norm[i]), bf(ple_w_gate[i]),
                  bf(ple_w_proj[i])]
        xc = _out(x1, gt, yr.reshape(T, RWKV_DIM), ya.reshape(T, ATTN_OUT_DIM), p[i].reshape(T, PLE_DIM), consts)
    return xc.reshape(B, S, D)
```

```python
import functools
import math

import numpy as np
import jax
import jax.numpy as jnp
from jax import lax
from jax.experimental import pallas as pl
from jax.experimental.pallas import tpu as pltpu

F32 = jnp.float32
BF16 = jnp.bfloat16

D_MODEL = 1024
PLE_DIM = 256
HEAD_DIM = 64
RWKV_HEADS = 8
RWKV_DIM = RWKV_HEADS * HEAD_DIM
DECAY_LORA = 64
ICLR_LORA = 64
GATE_LORA = 128
GN_EPS = 64e-5
ATTN_GROUPS = ((128, 1), (512, 4), (2048, 16))
HEADS_PER_GROUP = 4
ATTN_DIM = HEADS_PER_GROUP * len(ATTN_GROUPS) * HEAD_DIM
ATTN_OUT_DIM = HEADS_PER_GROUP * HEAD_DIM
BAND_BLOCK = 128
ROPE_THETA = 10000.0
NEG_INF = -1e30
D_FF = 2816
RMS_EPS = 1e-6
LOG2E = math.log2(math.e)
LN2 = math.log(2.0)
RWKV_COLS = 3 * RWKV_DIM + DECAY_LORA + ICLR_LORA + GATE_LORA
ATTN_COLS = 3 * ATTN_DIM
GATE_COLS = 2 * D_MODEL

LANES = 128
SUBLANES = 8
BF16_ROWS = 16
VMEM_LIMIT_BYTES = 60 * 1024 * 1024

FFN_TM = 512
PROJ_TM = 1024
OUT_TM = 512
RWKV_TS = 256
CHUNK = 64
CHUNKS_PER_ITER = 4
ATTN_ROWS = 256
ATTN_UNITS = 4


def _dot(a, b):
    return jnp.dot(a, b, preferred_element_type=F32)


def _dot_nt(a, b):
    return lax.dot_general(a, b, (((1,), (1,)), ((), ())), preferred_element_type=F32)


def _dot_tn(a, b):
    return lax.dot_general(a, b, (((0,), (0,)), ((), ())), preferred_element_type=F32)


def _rms(x, gain):
    return x * lax.rsqrt(jnp.mean(x * x, axis=-1, keepdims=True) + RMS_EPS) * gain


def _sigmoid(x):
    return 1.0 / (1.0 + jnp.exp(-x))


def _segsum(x, seg):
    w = seg.shape[0]
    parts = [_dot(x[:, lo:lo + w].astype(BF16), seg) for lo in range(0, x.shape[1], w)]
    return parts[0] if len(parts) == 1 else jnp.concatenate(parts, axis=1)


def _const_spec(arr):
    nd = arr.ndim
    return pl.BlockSpec(arr.shape, lambda *_: (0,) * nd, pipeline_mode=pl.Buffered(1))


def _params(*sem):
    return pltpu.CompilerParams(dimension_semantics=sem, vmem_limit_bytes=VMEM_LIMIT_BYTES)


def _cast_plan(weights, nsteps):
    specs, shapes = [], []
    for w in weights:
        rows, cols = w.shape
        every = 1
        while (rows * every) % nsteps or (rows * every // nsteps) % BF16_ROWS:
            every *= 2
        specs.append(pl.BlockSpec((rows * every // nsteps, cols), lambda i, every=every: (i // every, 0)))
        shapes.append(jax.ShapeDtypeStruct(w.shape, BF16))
    return specs, shapes


def _cast_slabs(src_refs, dst_refs):
    for src, dst in zip(src_refs, dst_refs):
        dst[...] = src[...].astype(BF16)


def _ffn1_kernel(ncast, x_ref, n1_ref, wg_ref, wu_ref, wd_ref, n2_ref, *rest):
    cast_in, (x1_ref, h2_ref), cast_out = rest[:ncast], rest[ncast:ncast + 2], rest[ncast + 2:]
    _cast_slabs(cast_in, cast_out)
    x = x_ref[...]
    h = _rms(x, n1_ref[...]).astype(BF16)
    g = _dot(h, wg_ref[...])
    u = _dot(h, wu_ref[...])
    act = (g * _sigmoid(g) * u).astype(BF16)
    x1 = x + 0.5 * _dot(act, wd_ref[...])
    x1_ref[...] = x1
    h2_ref[...] = _rms(x1, n2_ref[...]).astype(BF16)


def _ffn1(x2d, n1, wg, wu, wd, n2, next_weights):
    T = x2d.shape[0]
    nsteps = T // FFN_TM
    row = lambda i: (i, 0)
    cast_specs, cast_shapes = _cast_plan(next_weights, nsteps)
    return pl.pallas_call(
        functools.partial(_ffn1_kernel, len(next_weights)),
        grid=(nsteps,),
        in_specs=[pl.BlockSpec((FFN_TM, D_MODEL), row), _const_spec(n1), _const_spec(wg),
                  _const_spec(wu), _const_spec(wd), _const_spec(n2)] + cast_specs,
        out_specs=[pl.BlockSpec((FFN_TM, D_MODEL), row), pl.BlockSpec((FFN_TM, D_MODEL), row)] + cast_specs,
        out_shape=[jax.ShapeDtypeStruct((T, D_MODEL), F32), jax.ShapeDtypeStruct((T, D_MODEL), BF16)]
                  + cast_shapes,
        compiler_params=_params("arbitrary"),
        name="ffn1",
    )(x2d, n1, wg, wu, wd, n2, *next_weights)


def _inproj_kernel(ncast, h_ref, w_ref, *rest):
    cast_in, (zr_ref, za_ref, gt_ref), cast_out = rest[:ncast], rest[ncast:ncast + 3], rest[ncast + 3:]
    _cast_slabs(cast_in, cast_out)
    h = h_ref[...]
    zr_ref[...] = _dot(h, w_ref[:, :RWKV_COLS])
    za_ref[...] = _dot(h, w_ref[:, RWKV_COLS:RWKV_COLS + ATTN_COLS]).astype(BF16)
    gt_ref[...] = _sigmoid(_dot(h, w_ref[:, RWKV_COLS + ATTN_COLS:])).astype(BF16)


def _inproj(h2, w, next_weights):
    T = h2.shape[0]
    nsteps = T // PROJ_TM
    row = lambda i: (i, 0)
    cast_specs, cast_shapes = _cast_plan(next_weights, nsteps)
    return pl.pallas_call(
        functools.partial(_inproj_kernel, len(next_weights)),
        grid=(nsteps,),
        in_specs=[pl.BlockSpec((PROJ_TM, D_MODEL), row), _const_spec(w)] + cast_specs,
        out_specs=[pl.BlockSpec((PROJ_TM, RWKV_COLS), row), pl.BlockSpec((PROJ_TM, ATTN_COLS), row),
                   pl.BlockSpec((PROJ_TM, GATE_COLS), row)] + cast_specs,
        out_shape=[jax.ShapeDtypeStruct((T, RWKV_COLS), F32), jax.ShapeDtypeStruct((T, ATTN_COLS), BF16),
                   jax.ShapeDtypeStruct((T, GATE_COLS), BF16)] + cast_shapes,
        compiler_params=_params("arbitrary"),
        name="inproj",
    )(h2, w, *next_weights)


def _scan_chunks(row_list, h_ref, r_s, k_s, v_s, na_s, b_s, lw_s, cum_s, y_s):
    C = CHUNK
    C2 = 2 * C
    pairs = range(RWKV_HEADS // 2)
    rows_of = [rows for rows in row_list for _ in pairs]
    lanes = [slice(pr * LANES, (pr + 1) * LANES) for _ in row_list for pr in pairs]
    first = lax.broadcasted_iota(jnp.int32, (C, LANES), 1) < HEAD_DIM
    ri = lax.broadcasted_iota(jnp.int32, (C2, C2), 0)
    ci = lax.broadcasted_iota(jnp.int32, (C2, C2), 1)
    strict = ci < ri
    incl = ci <= ri
    diag = ci == ri
    eye = jnp.where(diag, 1.0, 0.0)

    def stack(x):
        return jnp.concatenate([jnp.where(first, x, 0.0), jnp.where(first, 0.0, x)], axis=0)

    def cat(parts, axis):
        return jnp.concatenate(parts, axis=axis)

    def load(ref):
        return [ref[rows, ln] for rows, ln in zip(rows_of, lanes)]

    lw = load(lw_s)
    cum = load(cum_s)
    p_inc = [jnp.exp(x) for x in cum]
    p_exc = [jnp.exp(x - y) for x, y in zip(cum, lw)]
    p_inv = [jnp.exp(-x) for x in cum]
    p_end = [x[C - 1:C, :] for x in p_inc]
    to_end = [x * y for x, y in zip(p_inv, p_end)]

    a_st = [stack(x * p) for x, p in zip(load(na_s), p_exc)]
    r_st = [stack(x * p) for x, p in zip(load(r_s), p_inc)]
    v_bf = [stack(x).astype(BF16) for x in load(v_s)]
    xr = [cat([a, r], 0).astype(BF16) for a, r in zip(a_st, r_st)]
    b = load(b_s)
    k = load(k_s)
    y_g = [cat([stack(bb * p), stack(kk * p)], 0).astype(BF16) for bb, kk, p in zip(b, k, p_inv)]
    y_h = [cat([stack(bb * p), stack(kk * p)], 0).astype(BF16) for bb, kk, p in zip(b, k, to_end)]

    g = [_dot_nt(x, y) for x, y in zip(xr, y_g)]
    l_ab = [jnp.where(strict, x[:C2, :C2], 0.0) for x in g]
    lmk = [cat([jnp.where(strict, x[:C2, C2:], 0.0), jnp.where(incl, x[C2:, C2:], 0.0)], 0).astype(BF16)
           for x in g]
    m_rb = [jnp.where(incl, x[C2:, :C2], 0.0).astype(BF16) for x in g]

    l_bf = [x.astype(BF16) for x in l_ab]
    pw = [_dot(x, x) for x in l_bf]
    acc = [eye + x for x in l_ab]
    for _ in range(1, int(math.log2(C)) - 1):
        ps = [_dot(p.astype(BF16), cat([s, p], 1).astype(BF16)) for p, s in zip(pw, acc)]
        acc = [s + x[:, :C2] for s, x in zip(acc, ps)]
        pw = [x[:, C2:] for x in ps]
    acc = [s + _dot(p.astype(BF16), s.astype(BF16)) for p, s in zip(pw, acc)]

    lm = [_dot(x, vv) for x, vv in zip(lmk, v_bf)]
    av = [_dot(t.astype(BF16), cat([a, x[:C2]], 1).astype(BF16)).astype(BF16)
          for t, a, x in zip(acc, a_st, lm)]
    ry = [_dot(m, x) + cat([r, y[C2:]], 1) for m, x, r, y in zip(m_rb, av, r_st, lm)]
    zero = jnp.zeros((C2, C2), BF16)
    md = [_dot_tn(yh, cat([x, cat([zero, vv], 1)], 0)) for yh, x, vv in zip(y_h, av, v_bf)]
    rmp = [cat([y[:, :C2], jnp.where(diag, jnp.broadcast_to(pe, (C2, C2)), 0.0) + x[:, :C2]], 0).astype(BF16)
           for y, pe, x in zip(ry, p_end, md)]

    h = [h_ref[pr] for pr in pairs]
    for ci_, rows in enumerate(row_list):
        base = ci_ * len(pairs)
        rm = [_dot(rmp[base + pr], h[pr].astype(BF16)) for pr in pairs]
        h = [rm[pr][C2:] + md[base + pr][:, C2:] for pr in pairs]
        for pr in pairs:
            y_st = rm[pr][:C2] + ry[base + pr][:, C2:]
            y_s[rows, lanes[pr]] = y_st[:C] + y_st[C:]
    for pr in pairs:
        h_ref[pr] = h[pr]


def _rwkv_kernel(z_ref, zh_ref, mu_ref, w0_ref, a0_ref, w2a_ref, g2_ref, kk_ref, ka_ref, rk_ref,
                 gnw_ref, gnb_ref, seg_ref, tril_ref, y_ref,
                 h_ref, r_s, k_s, v_s, na_s, b_s, lw_s, cum_s, g_s, y_s):
    t = pl.program_id(1)
    TS = RWKV_TS
    D = RWKV_DIM

    @pl.when(t == 0)
    def _():
        h_ref[...] = jnp.zeros_like(h_ref)

    has_prev = jnp.where(t > 0, 1.0, 0.0).astype(F32)

    def shifted(lo, hi):
        zc = z_ref[:, lo:hi]
        prev = zh_ref[SUBLANES - 1:SUBLANES, lo:hi] * has_prev
        row = lax.broadcasted_iota(jnp.int32, zc.shape, 0)
        zp = jnp.where(row == 0, prev, pltpu.roll(zc, 1, 0))
        return zc + (zp - zc) * mu_ref[:, lo:hi]

    seg = seg_ref[...]
    r = shifted(0, D)
    k = shifted(D, 2 * D)
    v = shifted(2 * D, 3 * D)
    wa = shifted(3 * D, 3 * D + DECAY_LORA + ICLR_LORA)
    gd = shifted(3 * D + DECAY_LORA + ICLR_LORA, RWKV_COLS)

    lane = lax.broadcasted_iota(jnp.int32, wa.shape, 1)
    lora_in = jnp.where(lane < DECAY_LORA, jnp.tanh(wa), wa).astype(BF16)
    lora = _dot(lora_in, w2a_ref[...])
    lw = -math.exp(-0.5) * _sigmoid(w0_ref[...] + lora[:, :D])
    lw_s[...] = lw
    tril = tril_ref[...]
    hi = lw.astype(BF16)
    lo = (lw - hi.astype(F32)).astype(BF16)
    cum_s[...] = _dot(tril, hi) + _dot(tril, lo)
    a = _sigmoid(a0_ref[...] + lora[:, D:])
    g_s[...] = _dot(_sigmoid(gd).astype(BF16), g2_ref[...])

    kk = k * kk_ref[...]
    kk = kk * lax.rsqrt(jnp.maximum(_segsum(kk * kk, seg), 1e-24))
    k2 = k * (1.0 + (a - 1.0) * ka_ref[...])
    r_s[...] = r
    k_s[...] = k2
    v_s[...] = v
    na_s[...] = -kk
    b_s[...] = kk * a

    def chunk_body(c, carry):
        row_list = [pl.ds(pl.multiple_of((c * CHUNKS_PER_ITER + i) * CHUNK, CHUNK), CHUNK)
                    for i in range(CHUNKS_PER_ITER)]
        _scan_chunks(row_list, h_ref, r_s, k_s, v_s, na_s, b_s, lw_s, cum_s, y_s)
        return carry

    lax.fori_loop(0, TS // (CHUNK * CHUNKS_PER_ITER), chunk_body, 0)

    y = y_s[...]
    inv_n = 1.0 / HEAD_DIM
    mean = _segsum(y, seg) * inv_n
    yc = y - mean
    var = _segsum(yc * yc, seg) * inv_n
    yn = yc * lax.rsqrt(var + GN_EPS) * gnw_ref[...] + gnb_ref[...]
    bonus = _segsum(r_s[...] * k_s[...] * rk_ref[...], seg) * v_s[...]
    y_ref[...] = ((yn + bonus) * g_s[...]).astype(BF16)


def _rwkv(zr, mu, w0, a0, w2a, g2, k_k, k_a, r_k, gn_w, gn_b, seg, B, S):
    TS = RWKV_TS
    t_idx = np.arange(TS)
    tril = jnp.asarray((t_idx[:, None] // CHUNK == t_idx[None, :] // CHUNK)
                       & (t_idx[None, :] <= t_idx[:, None]), BF16)
    consts = [mu, w0, a0, w2a, g2, k_k, k_a, r_k, gn_w, gn_b, seg, tril]
    halo_blocks = TS // SUBLANES
    stage = pltpu.VMEM((TS, RWKV_DIM), F32)
    return pl.pallas_call(
        _rwkv_kernel,
        grid=(B, S // TS),
        in_specs=[pl.BlockSpec((None, TS, RWKV_COLS), lambda b, t: (b, t, 0)),
                  pl.BlockSpec((None, SUBLANES, RWKV_COLS),
                               lambda b, t: (b, jnp.maximum(t * halo_blocks - 1, 0), 0))]
                 + [_const_spec(c) for c in consts],
        out_specs=pl.BlockSpec((None, TS, RWKV_DIM), lambda b, t: (b, t, 0)),
        out_shape=jax.ShapeDtypeStruct((B, S, RWKV_DIM), BF16),
        scratch_shapes=[pltpu.VMEM((RWKV_HEADS // 2, 2 * CHUNK, 2 * CHUNK), F32)] + [stage] * 9,
        compiler_params=_params("parallel", "arbitrary"),
        name="rwkv",
    )(zr, zr, *consts)


def _attn_group(gi, dil, S, x_s, o_s, l_s):
    L = S // dil
    nb = L // BAND_BLOCK
    Q = BAND_BLOCK
    qq = lax.broadcasted_iota(jnp.int32, (Q, Q), 0)
    kq = lax.broadcasted_iota(jnp.int32, (Q, Q), 1)
    lane2 = lax.broadcasted_iota(jnp.int32, (Q, 2 * LANES), 1)

    heads = range(HEADS_PER_GROUP)
    hmask = [lane2 // HEAD_DIM == j for j in heads]

    def load(slab, st):
        return jnp.concatenate([x_s[slab, pl.ds(st, Q, stride=dil), :],
                                x_s[slab + 1, pl.ds(st, Q, stride=dil), :]], axis=1).astype(BF16)

    def step(it, carry):
        starts, qs, keys, vals, valid = [], [], [], [], []
        for n in range(ATTN_UNITS):
            u = it * ATTN_UNITS + n
            res, blk = (u // nb, u % nb) if nb > 1 else (u, 0)
            start = res + blk * (Q * dil)
            starts.append(start)
            qs.append(load(0, start))
            kcur = load(2, start)
            vcur = load(4, start)
            if nb > 1:
                pstart = res + jnp.maximum(blk - 1, 0) * (Q * dil)
                keys.append(jnp.concatenate([load(2, pstart), kcur], axis=0))
                vals.append(jnp.concatenate([load(4, pstart), vcur], axis=0))
                off = jnp.where(blk > 0, 0, 2 * Q)
                valid.append(jnp.concatenate([kq >= qq + off, kq <= qq], axis=1))
            else:
                keys.append(kcur)
                vals.append(vcur)
                valid.append(kq <= qq)
        chains = [(n, j) for n in range(ATTN_UNITS) for j in heads]
        s = [_dot_nt(jnp.where(hmask[j], qs[n], jnp.zeros_like(qs[n])), keys[n]) for n, j in chains]
        s = [jnp.where(valid[n], x, NEG_INF) for (n, j), x in zip(chains, s)]
        m = [jnp.max(x, axis=-1, keepdims=True) for x in s]
        e = [jnp.exp2(x - y) for x, y in zip(s, m)]
        l = [jnp.sum(x, axis=-1, keepdims=True) for x in e]
        pv = [_dot(x.astype(BF16), vals[n]) for (n, j), x in zip(chains, e)]
        for n in range(ATTN_UNITS):
            o = jnp.zeros((Q, 2 * LANES), F32)
            lse = jnp.zeros((Q, 2 * LANES), F32)
            for j in heads:
                c = n * HEADS_PER_GROUP + j
                o = jnp.where(hmask[j], pv[c] / l[c], o)
                lse = jnp.where(hmask[j], m[c] * LN2 + jnp.log(l[c]), lse)
            rows = pl.ds(starts[n], Q, stride=dil)
            o_s[gi, 0, rows, :] = o[:, :LANES]
            o_s[gi, 1, rows, :] = o[:, LANES:]
            l_s[gi, 0, rows, :] = lse[:, :LANES]
            l_s[gi, 1, rows, :] = lse[:, LANES:]
        return carry

    lax.fori_loop(0, dil * nb // ATTN_UNITS, step, 0)


def _attn_kernel(q_ref, k_ref, v_ref, pos_ref, invf_ref, qg_ref, qgp_ref, kg_ref, kgp_ref, seg_ref, rot_ref,
                 out_ref, x_s, o_s, l_s, cs_s):
    g = pl.program_id(1)
    S = q_ref.shape[0]
    R = ATTN_ROWS

    @pl.when(g == 0)
    def _():
        def body(i, carry):
            rows = pl.ds(pl.multiple_of(i * R, R), R)
            ang = pos_ref[rows, :].astype(F32) * invf_ref[...]
            cs_s[0, rows, :] = jnp.cos(ang)
            cs_s[1, rows, :] = jnp.sin(ang)
            return carry
        lax.fori_loop(0, S // R, body, 0)

    seg = seg_ref[...]
    rot = rot_ref[...]

    def prep(i, carry):
        rows = pl.ds(pl.multiple_of(i * R, R), R)
        cos = cs_s[0, rows, :]
        sin = cs_s[1, rows, :]
        for src, slab, gain_ref, gainp_ref, scale in ((q_ref, 0, qg_ref, qgp_ref, HEAD_DIM ** -0.5 * LOG2E),
                                                      (k_ref, 2, kg_ref, kgp_ref, 1.0)):
            for hf in range(2):
                x = src[rows, hf * LANES:(hf + 1) * LANES]
                xf = x.astype(F32)
                inv = lax.rsqrt(_segsum(xf * xf, seg) * (1.0 / HEAD_DIM) + RMS_EPS) * scale
                xr = _dot(x, rot)
                x_s[slab + hf, rows, :] = (xf * gain_ref[...] * cos + xr * gainp_ref[...] * sin) * inv
        for hf in range(2):
            x_s[4 + hf, rows, :] = v_ref[rows, hf * LANES:(hf + 1) * LANES].astype(F32)
        return carry

    lax.fori_loop(0, S // R, prep, 0)

    for gi, (_, dil) in enumerate(ATTN_GROUPS):
        pl.when(g == gi)(functools.partial(_attn_group, gi, dil, S, x_s, o_s, l_s))

    @pl.when(g == len(ATTN_GROUPS) - 1)
    def _():
        def body(i, carry):
            rows = pl.ds(pl.multiple_of(i * R, R), R)
            for slab in range(2):
                ls = [l_s[gi, slab, rows, :] for gi in range(len(ATTN_GROUPS))]
                m = jnp.maximum(jnp.maximum(ls[0], ls[1]), ls[2])
                ws = [jnp.exp(x - m) for x in ls]
                num = sum(w * o_s[gi, slab, rows, :] for gi, w in enumerate(ws))
                out_ref[rows, slab * LANES:(slab + 1) * LANES] = (num / (ws[0] + ws[1] + ws[2])).astype(BF16)
            return carry
        lax.fori_loop(0, S // R, body, 0)


def _attn(za, pos, consts, B, S):
    ng = len(ATTN_GROUPS)
    gw = ATTN_OUT_DIM
    return pl.pallas_call(
        _attn_kernel,
        grid=(B, ng),
        in_specs=[pl.BlockSpec((None, S, gw), lambda b, g: (b, 0, g)),
                  pl.BlockSpec((None, S, gw), lambda b, g: (b, 0, ng + g)),
                  pl.BlockSpec((None, S, gw), lambda b, g: (b, 0, 2 * ng + g)),
                  pl.BlockSpec((None, S, 1), lambda b, g: (b, 0, 0))]
                 + [_const_spec(c) for c in consts],
        out_specs=pl.BlockSpec((None, S, ATTN_OUT_DIM), lambda b, g: (b, 0, 0)),
        out_shape=jax.ShapeDtypeStruct((B, S, ATTN_OUT_DIM), BF16),
        scratch_shapes=[pltpu.VMEM((6, S, LANES), F32), pltpu.VMEM((ng, 2, S, LANES), F32),
                        pltpu.VMEM((ng, 2, S, LANES), F32), pltpu.VMEM((2, S, LANES), F32)],
        compiler_params=_params("parallel", "arbitrary"),
        name="attn",
    )(za, za, za, pos, *consts)


def _out_kernel(x1_ref, gt_ref, yr_ref, ya_ref, p_ref, wbr_ref, wba_ref, wo_ref, n2_ref, wg_ref, wu_ref,
                wd_ref, pn_ref, pwg_ref, pwp_ref, out_ref):
    br = _dot(yr_ref[...], wbr_ref[...])
    ba = _dot(ya_ref[...], wba_ref[...])
    merged = gt_ref[:, :D_MODEL].astype(F32) * br + gt_ref[:, D_MODEL:].astype(F32) * ba
    x2 = x1_ref[...] + _dot(merged.astype(BF16), wo_ref[...])
    h = _rms(x2, n2_ref[...]).astype(BF16)
    g = _dot(h, wg_ref[...])
    u = _dot(h, wu_ref[...])
    act = (g * _sigmoid(g) * u).astype(BF16)
    x3 = x2 + 0.5 * _dot(act, wd_ref[...])
    hp = _rms(x3, pn_ref[...]).astype(BF16)
    gate = _sigmoid(_dot(hp, pwg_ref[...]))
    out_ref[...] = x3 + gate * _dot(p_ref[...].astype(BF16), pwp_ref[...])


def _out(x1, gt, yr, ya, p2d, consts):
    T = x1.shape[0]
    TM = OUT_TM
    row = lambda i: (i, 0)
    return pl.pallas_call(
        _out_kernel,
        grid=(T // TM,),
        in_specs=[pl.BlockSpec((TM, D_MODEL), row), pl.BlockSpec((TM, GATE_COLS), row),
                  pl.BlockSpec((TM, RWKV_DIM), row), pl.BlockSpec((TM, ATTN_OUT_DIM), row),
                  pl.BlockSpec((TM, PLE_DIM), row)] + [_const_spec(c) for c in consts],
        out_specs=pl.BlockSpec((TM, D_MODEL), row),
        out_shape=jax.ShapeDtypeStruct((T, D_MODEL), F32),
        compiler_params=_params("parallel"),
        name="merge_ffn2",
    )(x1, gt, yr, ya, p2d, *consts)


def _rotate_half_matrix():
    half = HEAD_DIM // 2
    m = np.zeros((LANES, LANES), np.float32)
    for lane in range(LANES):
        if lane % HEAD_DIM < half:
            m[lane + half, lane] = -1.0
        else:
            m[lane - half, lane] = 1.0
    return jnp.asarray(m, BF16)


def _block_ones(n, blk):
    idx = np.arange(n) // blk
    return jnp.asarray(idx[:, None] == idx[None, :], BF16)


def kernel(x, p, positions, ffn1_norm, ffn1_w_gate, ffn1_w_up, ffn1_w_down, mix_norm, w_in, rwkv_mu, rwkv_w0, rwkv_w2, rwkv_a0, rwkv_a2, rwkv_g2, rwkv_k_k, rwkv_k_a, rwkv_r_k, rwkv_gn_w, rwkv_gn_b, q_norm, k_norm, w_br_rwkv, w_br_attn, w_out, ffn2_norm, ffn2_w_gate, ffn2_w_up, ffn2_w_down, ple_norm, ple_w_gate, ple_w_proj):
    B, S, D = x.shape
    depth = p.shape[0]
    T = B * S
    half = HEAD_DIM // 2
    bf = lambda w: w.astype(BF16)
    rowvec = lambda a: a.reshape(1, -1).astype(F32)

    inv_freq = 1.0 / (ROPE_THETA ** (jnp.arange(0, HEAD_DIM, 2, dtype=F32) / HEAD_DIM))
    invf = jnp.tile(inv_freq, HEADS_PER_GROUP).reshape(1, LANES)
    pos = positions.reshape(B, S, 1)
    seg64 = _block_ones(2 * LANES, HEAD_DIM)
    seg_head = _block_ones(LANES, HEAD_DIM)
    rot = _rotate_half_matrix()
    heads_per_slab = LANES // HEAD_DIM
    gain = lambda a: jnp.tile(a, heads_per_slab).reshape(1, LANES).astype(F32)
    partner = lambda a: gain(jnp.concatenate([a[half:], a[:half]]))

    xc = x.reshape(T, D)
    for i in range(depth):
        zeros = jnp.zeros((DECAY_LORA, RWKV_DIM), F32)
        w2a = jnp.concatenate([jnp.concatenate([rwkv_w2[i], zeros], axis=1),
                               jnp.concatenate([zeros, rwkv_a2[i]], axis=1)], axis=0)

        x1, h2, w_in_bf = _ffn1(xc, rowvec(ffn1_norm[i]), bf(ffn1_w_gate[i]), bf(ffn1_w_up[i]), bf(ffn1_w_down[i]),
                                rowvec(mix_norm[i]), [w_in[i]])
        out_weights = [w_br_rwkv[i], w_br_attn[i], w_out[i], ffn2_w_gate[i], ffn2_w_up[i], ffn2_w_down[i],
                       ple_w_gate[i], ple_w_proj[i]]
        zr, za, gt, wbr, wba, wo, wg2, wu2, wd2, pwg, pwp = _inproj(h2, w_in_bf, out_weights)
        yr = _rwkv(zr.reshape(B, S, RWKV_COLS), rowvec(rwkv_mu[i]), rowvec(rwkv_w0[i]), rowvec(rwkv_a0[i]),
                   bf(w2a), bf(rwkv_g2[i]), rowvec(rwkv_k_k[i]), rowvec(rwkv_k_a[i]), rowvec(rwkv_r_k[i]),
                   rowvec(rwkv_gn_w[i]), rowvec(rwkv_gn_b[i]), seg64, B, S)
        attn_consts = [invf, gain(q_norm[i]), partner(q_norm[i]), gain(k_norm[i]), partner(k_norm[i]),
                       seg_head, rot]
        ya = _attn(za.reshape(B, S, ATTN_COLS), pos, attn_consts, B, S)
        consts = [wbr, wba, wo, rowvec(ffn2_norm[i]), wg2, wu2, wd2, rowvec(ple_norm[i]), pwg, pwp]
        xc = _out(x1, gt, yr.reshape(T, RWKV_DIM), ya.reshape(T, ATTN_OUT_DIM), p[i].reshape(T, PLE_DIM), consts)
    return xc.reshape(B, S, D)
```

```python
import functools
import math

import numpy as np
import jax
import jax.numpy as jnp
from jax import lax
from jax.experimental import pallas as pl
from jax.experimental.pallas import tpu as pltpu

F32 = jnp.float32
BF16 = jnp.bfloat16

D_MODEL = 1024
PLE_DIM = 256
HEAD_DIM = 64
RWKV_HEADS = 8
RWKV_DIM = RWKV_HEADS * HEAD_DIM
DECAY_LORA = 64
ICLR_LORA = 64
GATE_LORA = 128
GN_EPS = 64e-5
ATTN_GROUPS = ((128, 1), (512, 4), (2048, 16))
HEADS_PER_GROUP = 4
ATTN_DIM = HEADS_PER_GROUP * len(ATTN_GROUPS) * HEAD_DIM
ATTN_OUT_DIM = HEADS_PER_GROUP * HEAD_DIM
BAND_BLOCK = 128
ROPE_THETA = 10000.0
NEG_INF = -1e30
D_FF = 2816
RMS_EPS = 1e-6
LOG2E = math.log2(math.e)
LN2 = math.log(2.0)
RWKV_COLS = 3 * RWKV_DIM + DECAY_LORA + ICLR_LORA + GATE_LORA
ATTN_COLS = 3 * ATTN_DIM
GATE_COLS = 2 * D_MODEL

LANES = 128
SUBLANES = 8
BF16_ROWS = 16
VMEM_LIMIT_BYTES = 60 * 1024 * 1024

FFN_TM = 512
PROJ_TM = 1024
OUT_TM = 512
RWKV_TS = 256
CHUNK = 64
CHUNKS_PER_ITER = 4
ATTN_ROWS = 256
ATTN_UNITS = 16
FOLD_PITCH = BAND_BLOCK + SUBLANES
POS_PER_ROW = LANES // (HEAD_DIM // 2)


def _dot(a, b):
    return jnp.dot(a, b, preferred_element_type=F32)


def _dot_nt(a, b):
    return lax.dot_general(a, b, (((1,), (1,)), ((), ())), preferred_element_type=F32)


def _dot_tn(a, b):
    return lax.dot_general(a, b, (((0,), (0,)), ((), ())), preferred_element_type=F32)


def _rms(x, gain):
    return x * lax.rsqrt(jnp.mean(x * x, axis=-1, keepdims=True) + RMS_EPS) * gain


def _sigmoid(x):
    return 1.0 / (1.0 + jnp.exp(-x))


def _segsum(x, seg):
    w = seg.shape[0]
    parts = [_dot(x[:, lo:lo + w].astype(BF16), seg) for lo in range(0, x.shape[1], w)]
    return parts[0] if len(parts) == 1 else jnp.concatenate(parts, axis=1)


def _const_spec(arr):
    nd = arr.ndim
    return pl.BlockSpec(arr.shape, lambda *_: (0,) * nd, pipeline_mode=pl.Buffered(1))


def _params(*sem):
    return pltpu.CompilerParams(dimension_semantics=sem, vmem_limit_bytes=VMEM_LIMIT_BYTES)


def _cast_plan(weights, nsteps):
    specs, shapes = [], []
    for w in weights:
        rows, cols = w.shape
        every = 1
        while (rows * every) % nsteps or (rows * every // nsteps) % BF16_ROWS:
            every *= 2
        specs.append(pl.BlockSpec((rows * every // nsteps, cols), lambda i, every=every: (i // every, 0)))
        shapes.append(jax.ShapeDtypeStruct(w.shape, BF16))
    return specs, shapes


def _cast_slabs(src_refs, dst_refs):
    for src, dst in zip(src_refs, dst_refs):
        dst[...] = src[...].astype(BF16)


def _ffn1_kernel(ncast, x_ref, n1_ref, wg_ref, wu_ref, wd_ref, n2_ref, *rest):
    cast_in, (x1_ref, h2_ref), cast_out = rest[:ncast], rest[ncast:ncast + 2], rest[ncast + 2:]
    _cast_slabs(cast_in, cast_out)
    x = x_ref[...]
    h = _rms(x, n1_ref[...]).astype(BF16)
    g = _dot(h, wg_ref[...])
    u = _dot(h, wu_ref[...])
    act = (g * _sigmoid(g) * u).astype(BF16)
    x1 = x + 0.5 * _dot(act, wd_ref[...])
    x1_ref[...] = x1
    h2_ref[...] = _rms(x1, n2_ref[...]).astype(BF16)


def _ffn1(x2d, n1, wg, wu, wd, n2, next_weights):
    T = x2d.shape[0]
    nsteps = T // FFN_TM
    row = lambda i: (i, 0)
    cast_specs, cast_shapes = _cast_plan(next_weights, nsteps)
    return pl.pallas_call(
        functools.partial(_ffn1_kernel, len(next_weights)),
        grid=(nsteps,),
        in_specs=[pl.BlockSpec((FFN_TM, D_MODEL), row), _const_spec(n1), _const_spec(wg),
                  _const_spec(wu), _const_spec(wd), _const_spec(n2)] + cast_specs,
        out_specs=[pl.BlockSpec((FFN_TM, D_MODEL), row), pl.BlockSpec((FFN_TM, D_MODEL), row)] + cast_specs,
        out_shape=[jax.ShapeDtypeStruct((T, D_MODEL), F32), jax.ShapeDtypeStruct((T, D_MODEL), BF16)]
                  + cast_shapes,
        compiler_params=_params("arbitrary"),
        name="ffn1",
    )(x2d, n1, wg, wu, wd, n2, *next_weights)


def _inproj_kernel(ncast, h_ref, w_ref, *rest):
    cast_in, (zr_ref, za_ref, gt_ref), cast_out = rest[:ncast], rest[ncast:ncast + 3], rest[ncast + 3:]
    _cast_slabs(cast_in, cast_out)
    h = h_ref[...]
    zr_ref[...] = _dot(h, w_ref[:, :RWKV_COLS])
    za_ref[...] = _dot(h, w_ref[:, RWKV_COLS:RWKV_COLS + ATTN_COLS]).astype(BF16)
    gt_ref[...] = _sigmoid(_dot(h, w_ref[:, RWKV_COLS + ATTN_COLS:])).astype(BF16)


def _inproj(h2, w, next_weights):
    T = h2.shape[0]
    nsteps = T // PROJ_TM
    row = lambda i: (i, 0)
    cast_specs, cast_shapes = _cast_plan(next_weights, nsteps)
    return pl.pallas_call(
        functools.partial(_inproj_kernel, len(next_weights)),
        grid=(nsteps,),
        in_specs=[pl.BlockSpec((PROJ_TM, D_MODEL), row), _const_spec(w)] + cast_specs,
        out_specs=[pl.BlockSpec((PROJ_TM, RWKV_COLS), row), pl.BlockSpec((PROJ_TM, ATTN_COLS), row),
                   pl.BlockSpec((PROJ_TM, GATE_COLS), row)] + cast_specs,
        out_shape=[jax.ShapeDtypeStruct((T, RWKV_COLS), F32), jax.ShapeDtypeStruct((T, ATTN_COLS), BF16),
                   jax.ShapeDtypeStruct((T, GATE_COLS), BF16)] + cast_shapes,
        compiler_params=_params("arbitrary"),
        name="inproj",
    )(h2, w, *next_weights)


def _scan_chunks(row_list, h_ref, r_s, k_s, v_s, na_s, b_s, lw_s, cum_s, y_s):
    C = CHUNK
    C2 = 2 * C
    pairs = range(RWKV_HEADS // 2)
    rows_of = [rows for rows in row_list for _ in pairs]
    lanes = [slice(pr * LANES, (pr + 1) * LANES) for _ in row_list for pr in pairs]
    first = lax.broadcasted_iota(jnp.int32, (C, LANES), 1) < HEAD_DIM
    ri = lax.broadcasted_iota(jnp.int32, (C2, C2), 0)
    ci = lax.broadcasted_iota(jnp.int32, (C2, C2), 1)
    strict = ci < ri
    incl = ci <= ri
    diag = ci == ri
    eye = jnp.where(diag, 1.0, 0.0)

    def stack(x):
        return jnp.concatenate([jnp.where(first, x, 0.0), jnp.where(first, 0.0, x)], axis=0)

    def cat(parts, axis):
        return jnp.concatenate(parts, axis=axis)

    def load(ref):
        return [ref[rows, ln] for rows, ln in zip(rows_of, lanes)]

    lw = load(lw_s)
    cum = load(cum_s)
    p_inc = [jnp.exp(x) for x in cum]
    p_exc = [jnp.exp(x - y) for x, y in zip(cum, lw)]
    p_inv = [jnp.exp(-x) for x in cum]
    p_end = [x[C - 1:C, :] for x in p_inc]
    to_end = [x * y for x, y in zip(p_inv, p_end)]

    a_st = [stack(x * p) for x, p in zip(load(na_s), p_exc)]
    r_st = [stack(x * p) for x, p in zip(load(r_s), p_inc)]
    v_bf = [stack(x).astype(BF16) for x in load(v_s)]
    xr = [cat([a, r], 0).astype(BF16) for a, r in zip(a_st, r_st)]
    b = load(b_s)
    k = load(k_s)
    y_g = [cat([stack(bb * p), stack(kk * p)], 0).astype(BF16) for bb, kk, p in zip(b, k, p_inv)]
    y_h = [cat([stack(bb * p), stack(kk * p)], 0).astype(BF16) for bb, kk, p in zip(b, k, to_end)]

    g = [_dot_nt(x, y) for x, y in zip(xr, y_g)]
    l_ab = [jnp.where(strict, x[:C2, :C2], 0.0) for x in g]
    lmk = [cat([jnp.where(strict, x[:C2, C2:], 0.0), jnp.where(incl, x[C2:, C2:], 0.0)], 0).astype(BF16)
           for x in g]
    m_rb = [jnp.where(incl, x[C2:, :C2], 0.0).astype(BF16) for x in g]

    l_bf = [x.astype(BF16) for x in l_ab]
    pw = [_dot(x, x) for x in l_bf]
    acc = [eye + x for x in l_ab]
    for _ in range(1, int(math.log2(C)) - 1):
        ps = [_dot(p.astype(BF16), cat([s, p], 1).astype(BF16)) for p, s in zip(pw, acc)]
        acc = [s + x[:, :C2] for s, x in zip(acc, ps)]
        pw = [x[:, C2:] for x in ps]
    acc = [s + _dot(p.astype(BF16), s.astype(BF16)) for p, s in zip(pw, acc)]

    lm = [_dot(x, vv) for x, vv in zip(lmk, v_bf)]
    av = [_dot(t.astype(BF16), cat([a, x[:C2]], 1).astype(BF16)).astype(BF16)
          for t, a, x in zip(acc, a_st, lm)]
    ry = [_dot(m, x) + cat([r, y[C2:]], 1) for m, x, r, y in zip(m_rb, av, r_st, lm)]
    zero = jnp.zeros((C2, C2), BF16)
    md = [_dot_tn(yh, cat([x, cat([zero, vv], 1)], 0)) for yh, x, vv in zip(y_h, av, v_bf)]
    rmp = [cat([y[:, :C2], jnp.where(diag, jnp.broadcast_to(pe, (C2, C2)), 0.0) + x[:, :C2]], 0).astype(BF16)
           for y, pe, x in zip(ry, p_end, md)]

    h = [h_ref[pr] for pr in pairs]
    for ci_, rows in enumerate(row_list):
        base = ci_ * len(pairs)
        rm = [_dot(rmp[base + pr], h[pr].astype(BF16)) for pr in pairs]
        h = [rm[pr][C2:] + md[base + pr][:, C2:] for pr in pairs]
        for pr in pairs:
            y_st = rm[pr][:C2] + ry[base + pr][:, C2:]
            y_s[rows, lanes[pr]] = y_st[:C] + y_st[C:]
    for pr in pairs:
        h_ref[pr] = h[pr]


def _rwkv_kernel(z_ref, zh_ref, mu_ref, w0_ref, a0_ref, w2a_ref, g2_ref, kk_ref, ka_ref, rk_ref,
                 gnw_ref, gnb_ref, seg_ref, tril_ref, y_ref,
                 h_ref, r_s, k_s, v_s, na_s, b_s, lw_s, cum_s, g_s, y_s):
    t = pl.program_id(1)
    TS = RWKV_TS
    D = RWKV_DIM

    @pl.when(t == 0)
    def _():
        h_ref[...] = jnp.zeros_like(h_ref)

    has_prev = jnp.where(t > 0, 1.0, 0.0).astype(F32)

    def shifted(lo, hi):
        zc = z_ref[:, lo:hi]
        prev = zh_ref[SUBLANES - 1:SUBLANES, lo:hi] * has_prev
        row = lax.broadcasted_iota(jnp.int32, zc.shape, 0)
        zp = jnp.where(row == 0, prev, pltpu.roll(zc, 1, 0))
        return zc + (zp - zc) * mu_ref[:, lo:hi]

    seg = seg_ref[...]
    r = shifted(0, D)
    k = shifted(D, 2 * D)
    v = shifted(2 * D, 3 * D)
    wa = shifted(3 * D, 3 * D + DECAY_LORA + ICLR_LORA)
    gd = shifted(3 * D + DECAY_LORA + ICLR_LORA, RWKV_COLS)

    lane = lax.broadcasted_iota(jnp.int32, wa.shape, 1)
    lora_in = jnp.where(lane < DECAY_LORA, jnp.tanh(wa), wa).astype(BF16)
    lora = _dot(lora_in, w2a_ref[...])
    lw = -math.exp(-0.5) * _sigmoid(w0_ref[...] + lora[:, :D])
    lw_s[...] = lw
    tril = tril_ref[...]
    hi = lw.astype(BF16)
    lo = (lw - hi.astype(F32)).astype(BF16)
    cum_s[...] = _dot(tril, hi) + _dot(tril, lo)
    a = _sigmoid(a0_ref[...] + lora[:, D:])
    g_s[...] = _dot(_sigmoid(gd).astype(BF16), g2_ref[...])

    kk = k * kk_ref[...]
    kk = kk * lax.rsqrt(jnp.maximum(_segsum(kk * kk, seg), 1e-24))
    k2 = k * (1.0 + (a - 1.0) * ka_ref[...])
    r_s[...] = r
    k_s[...] = k2
    v_s[...] = v
    na_s[...] = -kk
    b_s[...] = kk * a

    def chunk_body(c, carry):
        row_list = [pl.ds(pl.multiple_of((c * CHUNKS_PER_ITER + i) * CHUNK, CHUNK), CHUNK)
                    for i in range(CHUNKS_PER_ITER)]
        _scan_chunks(row_list, h_ref, r_s, k_s, v_s, na_s, b_s, lw_s, cum_s, y_s)
        return carry

    lax.fori_loop(0, TS // (CHUNK * CHUNKS_PER_ITER), chunk_body, 0)

    y = y_s[...]
    inv_n = 1.0 / HEAD_DIM
    mean = _segsum(y, seg) * inv_n
    yc = y - mean
    var = _segsum(yc * yc, seg) * inv_n
    yn = yc * lax.rsqrt(var + GN_EPS) * gnw_ref[...] + gnb_ref[...]
    bonus = _segsum(r_s[...] * k_s[...] * rk_ref[...], seg) * v_s[...]
    y_ref[...] = ((yn + bonus) * g_s[...]).astype(BF16)


def _rwkv(zr, mu, w0, a0, w2a, g2, k_k, k_a, r_k, gn_w, gn_b, seg, B, S):
    TS = RWKV_TS
    t_idx = np.arange(TS)
    tril = jnp.asarray((t_idx[:, None] // CHUNK == t_idx[None, :] // CHUNK)
                       & (t_idx[None, :] <= t_idx[:, None]), BF16)
    consts = [mu, w0, a0, w2a, g2, k_k, k_a, r_k, gn_w, gn_b, seg, tril]
    halo_blocks = TS // SUBLANES
    stage = pltpu.VMEM((TS, RWKV_DIM), F32)
    return pl.pallas_call(
        _rwkv_kernel,
        grid=(B, S // TS),
        in_specs=[pl.BlockSpec((None, TS, RWKV_COLS), lambda b, t: (b, t, 0)),
                  pl.BlockSpec((None, SUBLANES, RWKV_COLS),
                               lambda b, t: (b, jnp.maximum(t * halo_blocks - 1, 0), 0))]
                 + [_const_spec(c) for c in consts],
        out_specs=pl.BlockSpec((None, TS, RWKV_DIM), lambda b, t: (b, t, 0)),
        out_shape=jax.ShapeDtypeStruct((B, S, RWKV_DIM), BF16),
        scratch_shapes=[pltpu.VMEM((RWKV_HEADS // 2, 2 * CHUNK, 2 * CHUNK), F32)] + [stage] * 9,
        compiler_params=_params("parallel", "arbitrary"),
        name="rwkv",
    )(zr, zr, *consts)


Q_SLAB, K_SLAB, V_SLAB, N_SLABS = 0, 4, 6, 10


def _fold_rows(tile, sub, dil):
    return pl.ds(tile * (ATTN_ROWS // dil) + sub, dil, stride=FOLD_PITCH)


def _attn_group(gi, dil, S, x_s, o_s, l_s, eye_ref, bias_ref):
    L = S // dil
    nb = L // BAND_BLOCK
    Q = BAND_BLOCK
    heads = range(HEADS_PER_GROUP)
    lane2 = lax.broadcasted_iota(jnp.int32, (Q, 2 * LANES), 1)
    hmask = [lane2 // HEAD_DIM == j for j in heads]
    eye = eye_ref[...]

    def load(slab, rows):
        return x_s[slab, rows, :].astype(BF16)

    def step(it, carry):
        cur, q, k_aug, v = [], [], [], []
        for n in range(ATTN_UNITS):
            u = it * ATTN_UNITS + n
            if nb > 1:
                res, blk = u // nb, u % nb
                rows = pl.ds(res + blk * (Q * dil), Q, stride=dil)
                prev = pl.ds(res + jnp.maximum(blk - 1, 0) * (Q * dil), Q, stride=dil)
                both = lambda slab, prev=prev, rows=rows: jnp.concatenate([load(slab, prev), load(slab, rows)],
                                                                          axis=0)
                bias = bias_ref[jnp.minimum(blk, 1)]
            else:
                rows = pl.ds(pl.multiple_of(u * FOLD_PITCH, SUBLANES), Q)
                both = lambda slab, rows=rows: load(slab, rows)
                bias = bias_ref[1, Q:, :]
            cur.append(rows)
            q.append([jnp.concatenate([load(Q_SLAB + j, rows), eye], axis=1) for j in heads])
            k_aug.append([jnp.concatenate([both(K_SLAB + hp), bias], axis=1) for hp in range(2)])
            v.append([both(V_SLAB + j) for j in heads])
        chains = [(n, j) for n in range(ATTN_UNITS) for j in heads]
        s = [_dot_nt(q[n][j], k_aug[n][j // 2]) for n, j in chains]
        m = [jnp.max(x, axis=-1, keepdims=True) for x in s]
        e = [jnp.exp2(x - y) for x, y in zip(s, m)]
        l = [jnp.sum(x, axis=-1, keepdims=True) for x in e]
        p = [(x * (1.0 / y)).astype(BF16) for x, y in zip(e, l)]
        for n in range(ATTN_UNITS):
            c = n * HEADS_PER_GROUP
            lse = jnp.zeros((Q, 2 * LANES), F32)
            for j in heads:
                lse = jnp.where(hmask[j], m[c + j] * LN2 + jnp.log(l[c + j]), lse)
            for hp in range(2):
                pp = jnp.concatenate([p[c + 2 * hp], p[c + 2 * hp + 1]], axis=1)
                vv = jnp.concatenate([v[n][2 * hp], v[n][2 * hp + 1]], axis=0)
                o_s[gi, hp, cur[n], :] = _dot(pp, vv)
                l_s[gi, hp, cur[n], :] = lse[:, hp * LANES:(hp + 1) * LANES]
        return carry

    lax.fori_loop(0, dil * nb // ATTN_UNITS, step, 0)


def _attn_kernel(q_ref, k_ref, v_ref, pos_ref, invf_ref, qg_ref, qgp_ref, kg_ref, kgp_ref, seg_ref, rot_ref,
                 eye_ref, bias_ref, spread_ref, out_ref, x_s, o_s, l_s, cs_s):
    g = pl.program_id(1)
    S = q_ref.shape[0]
    R = ATTN_ROWS
    ng = len(ATTN_GROUPS)
    fold_dil = ATTN_GROUPS[-1][1]

    @pl.when(g == 0)
    def _():
        rc = R

        def body(i, carry):
            ang = pos_ref[pl.ds(pl.multiple_of(i * rc, rc), rc), :].astype(F32) * invf_ref[...]
            for tbl, fn in enumerate((jnp.cos, jnp.sin)):
                val = fn(ang)
                hi = val.astype(BF16)
                rest = val - hi.astype(F32)
                mid = rest.astype(BF16)
                lo = (rest - mid.astype(F32)).astype(BF16)
                for sub in range(POS_PER_ROW):
                    spread = spread_ref[sub]
                    cs_s[tbl, pl.ds(i * (rc * POS_PER_ROW) + sub, rc, stride=POS_PER_ROW), :] = (
                        _dot(hi, spread) + _dot(mid, spread) + _dot(lo, spread))
            return carry
        lax.fori_loop(0, S // (rc * POS_PER_ROW), body, 0)

    seg = seg_ref[...]
    rot = rot_ref[...]
    first = lax.broadcasted_iota(jnp.int32, (R, LANES), 1) < HEAD_DIM

    def prep(folded, i, carry):
        rows = pl.ds(pl.multiple_of(i * R, R), R)
        cos = cs_s[0, rows, :]
        sin = cs_s[1, rows, :]

        def store(slab, val):
            if folded:
                for sub in range(R // fold_dil):
                    x_s[slab, _fold_rows(i, sub, fold_dil), :] = val[sub * fold_dil:(sub + 1) * fold_dil]
            else:
                x_s[slab, rows, :] = val

        def store_per_head(slab, val):
            store(slab, jnp.where(first, val, 0.0))
            store(slab + 1, jnp.where(first, 0.0, val))

        def normed(src, hf, gain_ref, gainp_ref, scale):
            x = src[rows, hf * LANES:(hf + 1) * LANES]
            xf = x.astype(F32)
            inv = lax.rsqrt(_segsum(xf * xf, seg) * (1.0 / HEAD_DIM) + RMS_EPS) * scale
            xr = _dot(x, rot)
            return (xf * gain_ref[...] * cos + xr * gainp_ref[...] * sin) * inv

        for hf in range(2):
            store_per_head(Q_SLAB + 2 * hf, normed(q_ref, hf, qg_ref, qgp_ref, HEAD_DIM ** -0.5 * LOG2E))
            store(K_SLAB + hf, normed(k_ref, hf, kg_ref, kgp_ref, 1.0))
            store_per_head(V_SLAB + 2 * hf, v_ref[rows, hf * LANES:(hf + 1) * LANES].astype(F32))
        return carry

    @pl.when(g < ng - 1)
    def _():
        lax.fori_loop(0, S // R, functools.partial(prep, False), 0)

    @pl.when(g == ng - 1)
    def _():
        lax.fori_loop(0, S // R, functools.partial(prep, True), 0)

    for gi, (_, dil) in enumerate(ATTN_GROUPS):
        pl.when(g == gi)(functools.partial(_attn_group, gi, dil, S, x_s, o_s, l_s, eye_ref, bias_ref))

    @pl.when(g == ng - 1)
    def _():
        def body(i, carry):
            rows = pl.ds(pl.multiple_of(i * R, R), R)

            def fetch(ref, gi, slab):
                if gi < ng - 1:
                    return ref[gi, slab, rows, :]
                return jnp.concatenate([ref[gi, slab, _fold_rows(i, sub, fold_dil), :]
                                        for sub in range(R // fold_dil)], axis=0)

            for slab in range(2):
                ls = [fetch(l_s, gi, slab) for gi in range(ng)]
                m = jnp.maximum(jnp.maximum(ls[0], ls[1]), ls[2])
                ws = [jnp.exp(x - m) for x in ls]
                num = sum(w * fetch(o_s, gi, slab) for gi, w in enumerate(ws))
                out_ref[rows, slab * LANES:(slab + 1) * LANES] = (num / (ws[0] + ws[1] + ws[2])).astype(BF16)
            return carry
        lax.fori_loop(0, S // R, body, 0)


def _attn(za, pos, consts, B, S):
    ng = len(ATTN_GROUPS)
    gw = ATTN_OUT_DIM
    rows = (S // BAND_BLOCK) * FOLD_PITCH
    return pl.pallas_call(
        _attn_kernel,
        grid=(B, ng),
        in_specs=[pl.BlockSpec((None, S, gw), lambda b, g: (b, 0, g)),
                  pl.BlockSpec((None, S, gw), lambda b, g: (b, 0, ng + g)),
                  pl.BlockSpec((None, S, gw), lambda b, g: (b, 0, 2 * ng + g)),
                  pl.BlockSpec((None, S // POS_PER_ROW, LANES), lambda b, g: (b, 0, 0))]
                 + [_const_spec(c) for c in consts],
        out_specs=pl.BlockSpec((None, S, ATTN_OUT_DIM), lambda b, g: (b, 0, 0)),
        out_shape=jax.ShapeDtypeStruct((B, S, ATTN_OUT_DIM), BF16),
        scratch_shapes=[pltpu.VMEM((N_SLABS, rows, LANES), F32), pltpu.VMEM((ng, 2, rows, LANES), F32),
                        pltpu.VMEM((ng, 2, rows, LANES), F32), pltpu.VMEM((2, S, LANES), F32)],
        compiler_params=_params("parallel", "arbitrary"),
        name="attn",
    )(za, za, za, pos, *consts)


def _out_kernel(x1_ref, gt_ref, yr_ref, ya_ref, p_ref, wbr_ref, wba_ref, wo_ref, n2_ref, wg_ref, wu_ref,
                wd_ref, pn_ref, pwg_ref, pwp_ref, out_ref):
    br = _dot(yr_ref[...], wbr_ref[...])
    ba = _dot(ya_ref[...], wba_ref[...])
    merged = gt_ref[:, :D_MODEL].astype(F32) * br + gt_ref[:, D_MODEL:].astype(F32) * ba
    x2 = x1_ref[...] + _dot(merged.astype(BF16), wo_ref[...])
    h = _rms(x2, n2_ref[...]).astype(BF16)
    g = _dot(h, wg_ref[...])
    u = _dot(h, wu_ref[...])
    act = (g * _sigmoid(g) * u).astype(BF16)
    x3 = x2 + 0.5 * _dot(act, wd_ref[...])
    hp = _rms(x3, pn_ref[...]).astype(BF16)
    gate = _sigmoid(_dot(hp, pwg_ref[...]))
    out_ref[...] = x3 + gate * _dot(p_ref[...].astype(BF16), pwp_ref[...])


def _out(x1, gt, yr, ya, p2d, consts):
    T = x1.shape[0]
    TM = OUT_TM
    row = lambda i: (i, 0)
    return pl.pallas_call(
        _out_kernel,
        grid=(T // TM,),
        in_specs=[pl.BlockSpec((TM, D_MODEL), row), pl.BlockSpec((TM, GATE_COLS), row),
                  pl.BlockSpec((TM, RWKV_DIM), row), pl.BlockSpec((TM, ATTN_OUT_DIM), row),
                  pl.BlockSpec((TM, PLE_DIM), row)] + [_const_spec(c) for c in consts],
        out_specs=pl.BlockSpec((TM, D_MODEL), row),
        out_shape=jax.ShapeDtypeStruct((T, D_MODEL), F32),
        compiler_params=_params("parallel"),
        name="merge_ffn2",
    )(x1, gt, yr, ya, p2d, *consts)


def _rotate_half_matrix():
    half = HEAD_DIM // 2
    m = np.zeros((LANES, LANES), np.float32)
    for lane in range(LANES):
        if lane % HEAD_DIM < half:
            m[lane + half, lane] = -1.0
        else:
            m[lane - half, lane] = 1.0
    return jnp.asarray(m, BF16)


def _band_bias():
    kq = np.arange(BAND_BLOCK)[:, None]
    qq = np.arange(BAND_BLOCK)[None, :]
    cur = np.where(kq <= qq, 0.0, NEG_INF)
    prev = np.where(kq >= qq, 0.0, NEG_INF)
    none = np.full_like(prev, NEG_INF)
    return jnp.asarray(np.stack([np.concatenate([none, cur]), np.concatenate([prev, cur])]), BF16)


def _block_ones(n, blk):
    idx = np.arange(n) // blk
    return jnp.asarray(idx[:, None] == idx[None, :], BF16)


def kernel(x, p, positions, ffn1_norm, ffn1_w_gate, ffn1_w_up, ffn1_w_down, mix_norm, w_in, rwkv_mu, rwkv_w0, rwkv_w2, rwkv_a0, rwkv_a2, rwkv_g2, rwkv_k_k, rwkv_k_a, rwkv_r_k, rwkv_gn_w, rwkv_gn_b, q_norm, k_norm, w_br_rwkv, w_br_attn, w_out, ffn2_norm, ffn2_w_gate, ffn2_w_up, ffn2_w_down, ple_norm, ple_w_gate, ple_w_proj):
    B, S, D = x.shape
    depth = p.shape[0]
    T = B * S
    half = HEAD_DIM // 2
    bf = lambda w: w.astype(BF16)
    rowvec = lambda a: a.reshape(1, -1).astype(F32)

    inv_freq = 1.0 / (ROPE_THETA ** (jnp.arange(0, HEAD_DIM, 2, dtype=F32) / HEAD_DIM))
    invf = jnp.tile(inv_freq, HEADS_PER_GROUP).reshape(1, LANES)
    pos = jnp.repeat(positions.reshape(B, S // POS_PER_ROW, POS_PER_ROW), half, axis=2)
    lane = np.arange(LANES)
    spread = jnp.asarray(np.stack([lane[:, None] == sub * half + lane[None, :] % half
                                   for sub in range(POS_PER_ROW)]), BF16)
    seg64 = _block_ones(2 * LANES, HEAD_DIM)
    seg_head = _block_ones(LANES, HEAD_DIM)
    rot = _rotate_half_matrix()
    eye = jnp.eye(BAND_BLOCK, dtype=BF16)
    heads_per_slab = LANES // HEAD_DIM
    gain = lambda a: jnp.tile(a, heads_per_slab).reshape(1, LANES).astype(F32)
    partner = lambda a: gain(jnp.concatenate([a[half:], a[:half]]))

    xc = x.reshape(T, D)
    for i in range(depth):
        zeros = jnp.zeros((DECAY_LORA, RWKV_DIM), F32)
        w2a = jnp.concatenate([jnp.concatenate([rwkv_w2[i], zeros], axis=1),
                               jnp.concatenate([zeros, rwkv_a2[i]], axis=1)], axis=0)

        x1, h2, w_in_bf = _ffn1(xc, rowvec(ffn1_norm[i]), bf(ffn1_w_gate[i]), bf(ffn1_w_up[i]), bf(ffn1_w_down[i]),
                                rowvec(mix_norm[i]), [w_in[i]])
        out_weights = [w_br_rwkv[i], w_br_attn[i], w_out[i], ffn2_w_gate[i], ffn2_w_up[i], ffn2_w_down[i],
                       ple_w_gate[i], ple_w_proj[i]]
        zr, za, gt, wbr, wba, wo, wg2, wu2, wd2, pwg, pwp = _inproj(h2, w_in_bf, out_weights)
        yr = _rwkv(zr.reshape(B, S, RWKV_COLS), rowvec(rwkv_mu[i]), rowvec(rwkv_w0[i]), rowvec(rwkv_a0[i]),
                   bf(w2a), bf(rwkv_g2[i]), rowvec(rwkv_k_k[i]), rowvec(rwkv_k_a[i]), rowvec(rwkv_r_k[i]),
                   rowvec(rwkv_gn_w[i]), rowvec(rwkv_gn_b[i]), seg64, B, S)
        attn_consts = [invf, gain(q_norm[i]), partner(q_norm[i]), gain(k_norm[i]), partner(k_norm[i]),
                       seg_head, rot, eye, _band_bias(), spread]
        ya = _attn(za.reshape(B, S, ATTN_COLS), pos, attn_consts, B, S)
        consts = [wbr, wba, wo, rowvec(ffn2_norm[i]), wg2, wu2, wd2, rowvec(ple_norm[i]), pwg, pwp]
        xc = _out(x1, gt, yr.reshape(T, RWKV_DIM), ya.reshape(T, ATTN_OUT_DIM), p[i].reshape(T, PLE_DIM), consts)
    return xc.reshape(B, S, D)
```

```python
import functools
import math

import numpy as np
import jax
import jax.numpy as jnp
from jax import lax
from jax.experimental import pallas as pl
from jax.experimental.pallas import tpu as pltpu

F32 = jnp.float32
BF16 = jnp.bfloat16

D_MODEL = 1024
PLE_DIM = 256
HEAD_DIM = 64
RWKV_HEADS = 8
RWKV_DIM = RWKV_HEADS * HEAD_DIM
DECAY_LORA = 64
ICLR_LORA = 64
GATE_LORA = 128
GN_EPS = 64e-5
ATTN_GROUPS = ((128, 1), (512, 4), (2048, 16))
HEADS_PER_GROUP = 4
ATTN_DIM = HEADS_PER_GROUP * len(ATTN_GROUPS) * HEAD_DIM
ATTN_OUT_DIM = HEADS_PER_GROUP * HEAD_DIM
BAND_BLOCK = 128
ROPE_THETA = 10000.0
NEG_INF = -1e30
D_FF = 2816
RMS_EPS = 1e-6
LOG2E = math.log2(math.e)
LN2 = math.log(2.0)
RWKV_COLS = 3 * RWKV_DIM + DECAY_LORA + ICLR_LORA + GATE_LORA
ATTN_COLS = 3 * ATTN_DIM
GATE_COLS = 2 * D_MODEL

LANES = 128
SUBLANES = 8
BF16_ROWS = 16
VMEM_LIMIT_BYTES = 60 * 1024 * 1024

FFN_TM = 512
PROJ_TM = 1024
OUT_TM = 512
RWKV_TS = 512
CHUNK = 64
CHUNKS_PER_ITER = 8
ATTN_ROWS = 256
ATTN_UNITS = 16
FOLD_PITCH = BAND_BLOCK + SUBLANES
POS_PER_ROW = LANES // (HEAD_DIM // 2)


def _dot(a, b):
    return jnp.dot(a, b, preferred_element_type=F32)


def _dot_nt(a, b):
    return lax.dot_general(a, b, (((1,), (1,)), ((), ())), preferred_element_type=F32)


def _dot_tn(a, b):
    return lax.dot_general(a, b, (((0,), (0,)), ((), ())), preferred_element_type=F32)


def _rms(x, gain):
    return x * lax.rsqrt(jnp.mean(x * x, axis=-1, keepdims=True) + RMS_EPS) * gain


def _sigmoid(x):
    return 1.0 / (1.0 + jnp.exp(-x))


def _segsum(x, seg):
    w = seg.shape[0]
    parts = [_dot(x[:, lo:lo + w].astype(BF16), seg) for lo in range(0, x.shape[1], w)]
    return parts[0] if len(parts) == 1 else jnp.concatenate(parts, axis=1)


def _const_spec(arr):
    nd = arr.ndim
    return pl.BlockSpec(arr.shape, lambda *_: (0,) * nd, pipeline_mode=pl.Buffered(1))


def _params(*sem):
    return pltpu.CompilerParams(dimension_semantics=sem, vmem_limit_bytes=VMEM_LIMIT_BYTES)


def _cast_plan(weights, nsteps):
    specs, shapes = [], []
    for w in weights:
        rows, cols = w.shape
        every = 1
        while (rows * every) % nsteps or (rows * every // nsteps) % BF16_ROWS:
            every *= 2
        specs.append(pl.BlockSpec((rows * every // nsteps, cols), lambda i, every=every: (i // every, 0)))
        shapes.append(jax.ShapeDtypeStruct(w.shape, BF16))
    return specs, shapes


def _cast_slabs(src_refs, dst_refs):
    for src, dst in zip(src_refs, dst_refs):
        dst[...] = src[...].astype(BF16)


def _ffn1_kernel(ncast, x_ref, n1_ref, wg_ref, wu_ref, wd_ref, n2_ref, *rest):
    cast_in, (x1_ref, h2_ref), cast_out = rest[:ncast], rest[ncast:ncast + 2], rest[ncast + 2:]
    _cast_slabs(cast_in, cast_out)
    x = x_ref[...]
    h = _rms(x, n1_ref[...]).astype(BF16)
    g = _dot(h, wg_ref[...])
    u = _dot(h, wu_ref[...])
    act = (g * _sigmoid(g) * u).astype(BF16)
    x1 = x + 0.5 * _dot(act, wd_ref[...])
    x1_ref[...] = x1
    h2_ref[...] = _rms(x1, n2_ref[...]).astype(BF16)


def _ffn1(x2d, n1, wg, wu, wd, n2, next_weights):
    T = x2d.shape[0]
    nsteps = T // FFN_TM
    row = lambda i: (i, 0)
    cast_specs, cast_shapes = _cast_plan(next_weights, nsteps)
    return pl.pallas_call(
        functools.partial(_ffn1_kernel, len(next_weights)),
        grid=(nsteps,),
        in_specs=[pl.BlockSpec((FFN_TM, D_MODEL), row), _const_spec(n1), _const_spec(wg),
                  _const_spec(wu), _const_spec(wd), _const_spec(n2)] + cast_specs,
        out_specs=[pl.BlockSpec((FFN_TM, D_MODEL), row), pl.BlockSpec((FFN_TM, D_MODEL), row)] + cast_specs,
        out_shape=[jax.ShapeDtypeStruct((T, D_MODEL), F32), jax.ShapeDtypeStruct((T, D_MODEL), BF16)]
                  + cast_shapes,
        compiler_params=_params("arbitrary"),
        name="ffn1",
    )(x2d, n1, wg, wu, wd, n2, *next_weights)


def _inproj_kernel(ncast, h_ref, w_ref, *rest):
    cast_in, (zr_ref, za_ref, gt_ref), cast_out = rest[:ncast], rest[ncast:ncast + 3], rest[ncast + 3:]
    _cast_slabs(cast_in, cast_out)
    h = h_ref[...]
    zr_ref[...] = _dot(h, w_ref[:, :RWKV_COLS])
    za_ref[...] = _dot(h, w_ref[:, RWKV_COLS:RWKV_COLS + ATTN_COLS]).astype(BF16)
    gt_ref[...] = _sigmoid(_dot(h, w_ref[:, RWKV_COLS + ATTN_COLS:])).astype(BF16)


def _inproj(h2, w, next_weights):
    T = h2.shape[0]
    nsteps = T // PROJ_TM
    row = lambda i: (i, 0)
    cast_specs, cast_shapes = _cast_plan(next_weights, nsteps)
    return pl.pallas_call(
        functools.partial(_inproj_kernel, len(next_weights)),
        grid=(nsteps,),
        in_specs=[pl.BlockSpec((PROJ_TM, D_MODEL), row), _const_spec(w)] + cast_specs,
        out_specs=[pl.BlockSpec((PROJ_TM, RWKV_COLS), row), pl.BlockSpec((PROJ_TM, ATTN_COLS), row),
                   pl.BlockSpec((PROJ_TM, GATE_COLS), row)] + cast_specs,
        out_shape=[jax.ShapeDtypeStruct((T, RWKV_COLS), F32), jax.ShapeDtypeStruct((T, ATTN_COLS), BF16),
                   jax.ShapeDtypeStruct((T, GATE_COLS), BF16)] + cast_shapes,
        compiler_params=_params("arbitrary"),
        name="inproj",
    )(h2, w, *next_weights)


def _scan_chunks(row_list, h_ref, r_s, k_s, v_s, na_s, b_s, lw_s, cum_s, y_s):
    C = CHUNK
    C2 = 2 * C
    pairs = range(RWKV_HEADS // 2)
    rows_of = [rows for rows in row_list for _ in pairs]
    lanes = [slice(pr * LANES, (pr + 1) * LANES) for _ in row_list for pr in pairs]
    first = lax.broadcasted_iota(jnp.int32, (C, LANES), 1) < HEAD_DIM
    ri = lax.broadcasted_iota(jnp.int32, (C2, C2), 0)
    ci = lax.broadcasted_iota(jnp.int32, (C2, C2), 1)
    strict = ci < ri
    incl = ci <= ri
    diag = ci == ri
    eye = jnp.where(diag, 1.0, 0.0)

    def stack(x):
        return jnp.concatenate([jnp.where(first, x, 0.0), jnp.where(first, 0.0, x)], axis=0)

    def cat(parts, axis):
        return jnp.concatenate(parts, axis=axis)

    def load(ref):
        return [ref[rows, ln] for rows, ln in zip(rows_of, lanes)]

    lw = load(lw_s)
    cum = load(cum_s)
    p_inc = [jnp.exp(x) for x in cum]
    p_exc = [jnp.exp(x - y) for x, y in zip(cum, lw)]
    p_inv = [jnp.exp(-x) for x in cum]
    p_end = [x[C - 1:C, :] for x in p_inc]
    to_end = [x * y for x, y in zip(p_inv, p_end)]

    a_st = [stack(x * p) for x, p in zip(load(na_s), p_exc)]
    r_st = [stack(x * p) for x, p in zip(load(r_s), p_inc)]
    v_bf = [stack(x).astype(BF16) for x in load(v_s)]
    xr = [cat([a, r], 0).astype(BF16) for a, r in zip(a_st, r_st)]
    b = load(b_s)
    k = load(k_s)
    y_g = [cat([stack(bb * p), stack(kk * p)], 0).astype(BF16) for bb, kk, p in zip(b, k, p_inv)]
    y_h = [cat([stack(bb * p), stack(kk * p)], 0).astype(BF16) for bb, kk, p in zip(b, k, to_end)]

    g = [_dot_nt(x, y) for x, y in zip(xr, y_g)]
    l_ab = [jnp.where(strict, x[:C2, :C2], 0.0) for x in g]
    lmk = [cat([jnp.where(strict, x[:C2, C2:], 0.0), jnp.where(incl, x[C2:, C2:], 0.0)], 0).astype(BF16)
           for x in g]
    m_rb = [jnp.where(incl, x[C2:, :C2], 0.0).astype(BF16) for x in g]

    l_bf = [x.astype(BF16) for x in l_ab]
    pw = [_dot(x, x) for x in l_bf]
    acc = [eye + x for x in l_ab]
    for _ in range(1, int(math.log2(C)) - 1):
        ps = [_dot(p.astype(BF16), cat([s, p], 1).astype(BF16)) for p, s in zip(pw, acc)]
        acc = [s + x[:, :C2] for s, x in zip(acc, ps)]
        pw = [x[:, C2:] for x in ps]
    acc = [s + _dot(p.astype(BF16), s.astype(BF16)) for p, s in zip(pw, acc)]

    lm = [_dot(x, vv) for x, vv in zip(lmk, v_bf)]
    av = [_dot(t.astype(BF16), cat([a, x[:C2]], 1).astype(BF16)).astype(BF16)
          for t, a, x in zip(acc, a_st, lm)]
    ry = [_dot(m, x) + cat([r, y[C2:]], 1) for m, x, r, y in zip(m_rb, av, r_st, lm)]
    zero = jnp.zeros((C2, C2), BF16)
    md = [_dot_tn(yh, cat([x, cat([zero, vv], 1)], 0)) for yh, x, vv in zip(y_h, av, v_bf)]
    rmp = [cat([y[:, :C2], jnp.where(diag, jnp.broadcast_to(pe, (C2, C2)), 0.0) + x[:, :C2]], 0).astype(BF16)
           for y, pe, x in zip(ry, p_end, md)]

    h = [h_ref[pr] for pr in pairs]
    for ci_, rows in enumerate(row_list):
        base = ci_ * len(pairs)
        rm = [_dot(rmp[base + pr], h[pr].astype(BF16)) for pr in pairs]
        h = [rm[pr][C2:] + md[base + pr][:, C2:] for pr in pairs]
        for pr in pairs:
            y_st = rm[pr][:C2] + ry[base + pr][:, C2:]
            y_s[rows, lanes[pr]] = y_st[:C] + y_st[C:]
    for pr in pairs:
        h_ref[pr] = h[pr]


def _rwkv_kernel(z_ref, zh_ref, mu_ref, w0_ref, a0_ref, w2a_ref, g2_ref, kk_ref, ka_ref, rk_ref,
                 gnw_ref, gnb_ref, seg_ref, tril_ref, y_ref,
                 h_ref, r_s, k_s, v_s, na_s, b_s, lw_s, cum_s, g_s, y_s):
    t = pl.program_id(1)
    TS = RWKV_TS
    D = RWKV_DIM

    @pl.when(t == 0)
    def _():
        h_ref[...] = jnp.zeros_like(h_ref)

    has_prev = jnp.where(t > 0, 1.0, 0.0).astype(F32)

    def shifted(lo, hi):
        zc = z_ref[:, lo:hi]
        prev = zh_ref[SUBLANES - 1:SUBLANES, lo:hi] * has_prev
        row = lax.broadcasted_iota(jnp.int32, zc.shape, 0)
        zp = jnp.where(row == 0, prev, pltpu.roll(zc, 1, 0))
        return zc + (zp - zc) * mu_ref[:, lo:hi]

    seg = seg_ref[...]
    r = shifted(0, D)
    k = shifted(D, 2 * D)
    v = shifted(2 * D, 3 * D)
    wa = shifted(3 * D, 3 * D + DECAY_LORA + ICLR_LORA)
    gd = shifted(3 * D + DECAY_LORA + ICLR_LORA, RWKV_COLS)

    lane = lax.broadcasted_iota(jnp.int32, wa.shape, 1)
    lora_in = jnp.where(lane < DECAY_LORA, jnp.tanh(wa), wa).astype(BF16)
    lora = _dot(lora_in, w2a_ref[...])
    lw = -math.exp(-0.5) * _sigmoid(w0_ref[...] + lora[:, :D])
    lw_s[...] = lw
    tril = tril_ref[...]
    hi = lw.astype(BF16)
    lo = (lw - hi.astype(F32)).astype(BF16)
    cum_s[...] = _dot(tril, hi) + _dot(tril, lo)
    a = _sigmoid(a0_ref[...] + lora[:, D:])
    g_s[...] = _dot(_sigmoid(gd).astype(BF16), g2_ref[...])

    kk = k * kk_ref[...]
    kk = kk * lax.rsqrt(jnp.maximum(_segsum(kk * kk, seg), 1e-24))
    k2 = k * (1.0 + (a - 1.0) * ka_ref[...])
    r_s[...] = r
    k_s[...] = k2
    v_s[...] = v
    na_s[...] = -kk
    b_s[...] = kk * a

    def chunk_body(c, carry):
        row_list = [pl.ds(pl.multiple_of((c * CHUNKS_PER_ITER + i) * CHUNK, CHUNK), CHUNK)
                    for i in range(CHUNKS_PER_ITER)]
        _scan_chunks(row_list, h_ref, r_s, k_s, v_s, na_s, b_s, lw_s, cum_s, y_s)
        return carry

    lax.fori_loop(0, TS // (CHUNK * CHUNKS_PER_ITER), chunk_body, 0)

    y = y_s[...]
    inv_n = 1.0 / HEAD_DIM
    mean = _segsum(y, seg) * inv_n
    yc = y - mean
    var = _segsum(yc * yc, seg) * inv_n
    yn = yc * lax.rsqrt(var + GN_EPS) * gnw_ref[...] + gnb_ref[...]
    bonus = _segsum(r_s[...] * k_s[...] * rk_ref[...], seg) * v_s[...]
    y_ref[...] = ((yn + bonus) * g_s[...]).astype(BF16)


def _rwkv(zr, mu, w0, a0, w2a, g2, k_k, k_a, r_k, gn_w, gn_b, seg, B, S):
    TS = RWKV_TS
    t_idx = np.arange(TS)
    tril = jnp.asarray((t_idx[:, None] // CHUNK == t_idx[None, :] // CHUNK)
                       & (t_idx[None, :] <= t_idx[:, None]), BF16)
    consts = [mu, w0, a0, w2a, g2, k_k, k_a, r_k, gn_w, gn_b, seg, tril]
    halo_blocks = TS // SUBLANES
    stage = pltpu.VMEM((TS, RWKV_DIM), F32)
    return pl.pallas_call(
        _rwkv_kernel,
        grid=(B, S // TS),
        in_specs=[pl.BlockSpec((None, TS, RWKV_COLS), lambda b, t: (b, t, 0)),
                  pl.BlockSpec((None, SUBLANES, RWKV_COLS),
                               lambda b, t: (b, jnp.maximum(t * halo_blocks - 1, 0), 0))]
                 + [_const_spec(c) for c in consts],
        out_specs=pl.BlockSpec((None, TS, RWKV_DIM), lambda b, t: (b, t, 0)),
        out_shape=jax.ShapeDtypeStruct((B, S, RWKV_DIM), BF16),
        scratch_shapes=[pltpu.VMEM((RWKV_HEADS // 2, 2 * CHUNK, 2 * CHUNK), F32)] + [stage] * 9,
        compiler_params=_params("parallel", "arbitrary"),
        name="rwkv",
    )(zr, zr, *consts)


Q_SLAB, K_SLAB, V_SLAB, N_SLABS = 0, 4, 6, 10


def _fold_rows(tile, sub, dil):
    return pl.ds(tile * (ATTN_ROWS // dil) + sub, dil, stride=FOLD_PITCH)


def _attn_group(gi, dil, S, x_s, o_s, l_s, eye_ref, bias_ref):
    L = S // dil
    nb = L // BAND_BLOCK
    Q = BAND_BLOCK
    heads = range(HEADS_PER_GROUP)
    lane2 = lax.broadcasted_iota(jnp.int32, (Q, 2 * LANES), 1)
    hmask = [lane2 // HEAD_DIM == j for j in heads]
    eye = eye_ref[...]

    def load(slab, rows):
        return x_s[slab, rows, :].astype(BF16)

    def step(it, carry):
        cur, q, k_aug, v = [], [], [], []
        for n in range(ATTN_UNITS):
            u = it * ATTN_UNITS + n
            if nb > 1:
                res, blk = u // nb, u % nb
                rows = pl.ds(res + blk * (Q * dil), Q, stride=dil)
                prev = pl.ds(res + jnp.maximum(blk - 1, 0) * (Q * dil), Q, stride=dil)
                both = lambda slab, prev=prev, rows=rows: jnp.concatenate([load(slab, prev), load(slab, rows)],
                                                                          axis=0)
                bias = bias_ref[jnp.minimum(blk, 1)]
            else:
                rows = pl.ds(pl.multiple_of(u * FOLD_PITCH, SUBLANES), Q)
                both = lambda slab, rows=rows: load(slab, rows)
                bias = bias_ref[1, Q:, :]
            cur.append(rows)
            q.append([jnp.concatenate([load(Q_SLAB + j, rows), eye], axis=1) for j in heads])
            k_aug.append([jnp.concatenate([both(K_SLAB + hp), bias], axis=1) for hp in range(2)])
            v.append([both(V_SLAB + j) for j in heads])
        chains = [(n, j) for n in range(ATTN_UNITS) for j in heads]
        s = [_dot_nt(q[n][j], k_aug[n][j // 2]) for n, j in chains]
        m = [jnp.max(x, axis=-1, keepdims=True) for x in s]
        e = [jnp.exp2(x - y) for x, y in zip(s, m)]
        l = [jnp.sum(x, axis=-1, keepdims=True) for x in e]
        p = [(x * (1.0 / y)).astype(BF16) for x, y in zip(e, l)]
        for n in range(ATTN_UNITS):
            c = n * HEADS_PER_GROUP
            lse = jnp.zeros((Q, 2 * LANES), F32)
            for j in heads:
                lse = jnp.where(hmask[j], m[c + j] * LN2 + jnp.log(l[c + j]), lse)
            for hp in range(2):
                pp = jnp.concatenate([p[c + 2 * hp], p[c + 2 * hp + 1]], axis=1)
                vv = jnp.concatenate([v[n][2 * hp], v[n][2 * hp + 1]], axis=0)
                o_s[gi, hp, cur[n], :] = _dot(pp, vv)
                l_s[gi, hp, cur[n], :] = lse[:, hp * LANES:(hp + 1) * LANES]
        return carry

    lax.fori_loop(0, dil * nb // ATTN_UNITS, step, 0)


def _attn_kernel(q_ref, k_ref, v_ref, pos_ref, invf_ref, qg_ref, qgp_ref, kg_ref, kgp_ref, seg_ref, rot_ref,
                 eye_ref, bias_ref, spread_ref, out_ref, x_s, o_s, l_s, cs_s):
    g = pl.program_id(1)
    S = q_ref.shape[0]
    R = ATTN_ROWS
    ng = len(ATTN_GROUPS)
    fold_dil = ATTN_GROUPS[-1][1]

    @pl.when(g == 0)
    def _():
        rc = R

        def body(i, carry):
            ang = pos_ref[pl.ds(pl.multiple_of(i * rc, rc), rc), :].astype(F32) * invf_ref[...]
            for tbl, fn in enumerate((jnp.cos, jnp.sin)):
                val = fn(ang)
                hi = val.astype(BF16)
                rest = val - hi.astype(F32)
                mid = rest.astype(BF16)
                lo = (rest - mid.astype(F32)).astype(BF16)
                q_gain, k_gain = ((qg_ref, kg_ref), (qgp_ref, kgp_ref))[tbl]
                for sub in range(POS_PER_ROW):
                    spread = spread_ref[sub]
                    full = _dot(hi, spread) + _dot(mid, spread) + _dot(lo, spread)
                    rows = pl.ds(i * (rc * POS_PER_ROW) + sub, rc, stride=POS_PER_ROW)
                    cs_s[tbl, rows, :] = full * (q_gain[...] * (HEAD_DIM ** -0.5 * LOG2E))
                    cs_s[2 + tbl, rows, :] = full * k_gain[...]
            return carry
        lax.fori_loop(0, S // (rc * POS_PER_ROW), body, 0)

    seg = seg_ref[...]
    rot = rot_ref[...]
    first = lax.broadcasted_iota(jnp.int32, (R, LANES), 1) < HEAD_DIM

    def prep(folded, i, carry):
        rows = pl.ds(pl.multiple_of(i * R, R), R)

        def store(slab, val):
            if folded:
                for sub in range(R // fold_dil):
                    x_s[slab, _fold_rows(i, sub, fold_dil), :] = val[sub * fold_dil:(sub + 1) * fold_dil]
            else:
                x_s[slab, rows, :] = val

        def store_per_head(slab, val):
            store(slab, jnp.where(first, val, 0.0))
            store(slab + 1, jnp.where(first, 0.0, val))

        def normed(src, hf, tables):
            x = src[rows, hf * LANES:(hf + 1) * LANES]
            xf = x.astype(F32)
            inv = lax.rsqrt(_segsum(xf * xf, seg) + RMS_EPS)
            xr = _dot(x, rot)
            return (xf * cs_s[tables, rows, :] + xr * cs_s[tables + 1, rows, :]) * inv

        for hf in range(2):
            store_per_head(Q_SLAB + 2 * hf, normed(q_ref, hf, 0))
            store(K_SLAB + hf, normed(k_ref, hf, 2))
            store_per_head(V_SLAB + 2 * hf, v_ref[rows, hf * LANES:(hf + 1) * LANES].astype(F32))
        return carry

    @pl.when(g < ng - 1)
    def _():
        lax.fori_loop(0, S // R, functools.partial(prep, False), 0)

    @pl.when(g == ng - 1)
    def _():
        lax.fori_loop(0, S // R, functools.partial(prep, True), 0)

    for gi, (_, dil) in enumerate(ATTN_GROUPS):
        pl.when(g == gi)(functools.partial(_attn_group, gi, dil, S, x_s, o_s, l_s, eye_ref, bias_ref))

    @pl.when(g == ng - 1)
    def _():
        def body(i, carry):
            rows = pl.ds(pl.multiple_of(i * R, R), R)

            def fetch(ref, gi, slab):
                if gi < ng - 1:
                    return ref[gi, slab, rows, :]
                return jnp.concatenate([ref[gi, slab, _fold_rows(i, sub, fold_dil), :]
                                        for sub in range(R // fold_dil)], axis=0)

            for slab in range(2):
                ls = [fetch(l_s, gi, slab) for gi in range(ng)]
                m = jnp.maximum(jnp.maximum(ls[0], ls[1]), ls[2])
                ws = [jnp.exp(x - m) for x in ls]
                num = sum(w * fetch(o_s, gi, slab) for gi, w in enumerate(ws))
                out_ref[rows, slab * LANES:(slab + 1) * LANES] = (num / (ws[0] + ws[1] + ws[2])).astype(BF16)
            return carry
        lax.fori_loop(0, S // R, body, 0)


def _attn(za, pos, consts, B, S):
    ng = len(ATTN_GROUPS)
    gw = ATTN_OUT_DIM
    rows = (S // BAND_BLOCK) * FOLD_PITCH
    return pl.pallas_call(
        _attn_kernel,
        grid=(B, ng),
        in_specs=[pl.BlockSpec((None, S, gw), lambda b, g: (b, 0, g)),
                  pl.BlockSpec((None, S, gw), lambda b, g: (b, 0, ng + g)),
                  pl.BlockSpec((None, S, gw), lambda b, g: (b, 0, 2 * ng + g)),
                  pl.BlockSpec((None, S // POS_PER_ROW, LANES), lambda b, g: (b, 0, 0))]
                 + [_const_spec(c) for c in consts],
        out_specs=pl.BlockSpec((None, S, ATTN_OUT_DIM), lambda b, g: (b, 0, 0)),
        out_shape=jax.ShapeDtypeStruct((B, S, ATTN_OUT_DIM), BF16),
        scratch_shapes=[pltpu.VMEM((N_SLABS, rows, LANES), F32), pltpu.VMEM((ng, 2, rows, LANES), F32),
                        pltpu.VMEM((ng, 2, rows, LANES), F32), pltpu.VMEM((4, S, LANES), F32)],
        compiler_params=_params("parallel", "arbitrary"),
        name="attn",
    )(za, za, za, pos, *consts)


def _out_kernel(x1_ref, gt_ref, yr_ref, ya_ref, p_ref, wbr_ref, wba_ref, wo_ref, n2_ref, wg_ref, wu_ref,
                wd_ref, pn_ref, pwg_ref, pwp_ref, out_ref):
    br = _dot(yr_ref[...], wbr_ref[...])
    ba = _dot(ya_ref[...], wba_ref[...])
    merged = gt_ref[:, :D_MODEL].astype(F32) * br + gt_ref[:, D_MODEL:].astype(F32) * ba
    x2 = x1_ref[...] + _dot(merged.astype(BF16), wo_ref[...])
    h = _rms(x2, n2_ref[...]).astype(BF16)
    g = _dot(h, wg_ref[...])
    u = _dot(h, wu_ref[...])
    act = (g * _sigmoid(g) * u).astype(BF16)
    x3 = x2 + 0.5 * _dot(act, wd_ref[...])
    hp = _rms(x3, pn_ref[...]).astype(BF16)
    gate = _sigmoid(_dot(hp, pwg_ref[...]))
    out_ref[...] = x3 + gate * _dot(p_ref[...].astype(BF16), pwp_ref[...])


def _out(x1, gt, yr, ya, p2d, consts):
    T = x1.shape[0]
    TM = OUT_TM
    row = lambda i: (i, 0)
    return pl.pallas_call(
        _out_kernel,
        grid=(T // TM,),
        in_specs=[pl.BlockSpec((TM, D_MODEL), row), pl.BlockSpec((TM, GATE_COLS), row),
                  pl.BlockSpec((TM, RWKV_DIM), row), pl.BlockSpec((TM, ATTN_OUT_DIM), row),
                  pl.BlockSpec((TM, PLE_DIM), row)] + [_const_spec(c) for c in consts],
        out_specs=pl.BlockSpec((TM, D_MODEL), row),
        out_shape=jax.ShapeDtypeStruct((T, D_MODEL), F32),
        compiler_params=_params("parallel"),
        name="merge_ffn2",
    )(x1, gt, yr, ya, p2d, *consts)


def _rotate_half_matrix():
    half = HEAD_DIM // 2
    m = np.zeros((LANES, LANES), np.float32)
    for lane in range(LANES):
        if lane % HEAD_DIM < half:
            m[lane + half, lane] = -1.0
        else:
            m[lane - half, lane] = 1.0
    return jnp.asarray(m, BF16)


def _band_bias():
    kq = np.arange(BAND_BLOCK)[:, None]
    qq = np.arange(BAND_BLOCK)[None, :]
    cur = np.where(kq <= qq, 0.0, NEG_INF)
    prev = np.where(kq >= qq, 0.0, NEG_INF)
    none = np.full_like(prev, NEG_INF)
    return jnp.asarray(np.stack([np.concatenate([none, cur]), np.concatenate([prev, cur])]), BF16)


def _block_ones(n, blk):
    idx = np.arange(n) // blk
    return jnp.asarray(idx[:, None] == idx[None, :], BF16)


def kernel(x, p, positions, ffn1_norm, ffn1_w_gate, ffn1_w_up, ffn1_w_down, mix_norm, w_in, rwkv_mu, rwkv_w0, rwkv_w2, rwkv_a0, rwkv_a2, rwkv_g2, rwkv_k_k, rwkv_k_a, rwkv_r_k, rwkv_gn_w, rwkv_gn_b, q_norm, k_norm, w_br_rwkv, w_br_attn, w_out, ffn2_norm, ffn2_w_gate, ffn2_w_up, ffn2_w_down, ple_norm, ple_w_gate, ple_w_proj):
    B, S, D = x.shape
    depth = p.shape[0]
    T = B * S
    half = HEAD_DIM // 2
    bf = lambda w: w.astype(BF16)
    rowvec = lambda a: a.reshape(1, -1).astype(F32)

    inv_freq = 1.0 / (ROPE_THETA ** (jnp.arange(0, HEAD_DIM, 2, dtype=F32) / HEAD_DIM))
    invf = jnp.tile(inv_freq, HEADS_PER_GROUP).reshape(1, LANES)
    pos = jnp.repeat(positions.reshape(B, S // POS_PER_ROW, POS_PER_ROW), half, axis=2)
    lane = np.arange(LANES)
    spread = jnp.asarray(np.stack([lane[:, None] == sub * half + lane[None, :] % half
                                   for sub in range(POS_PER_ROW)]), BF16)
    seg64 = _block_ones(2 * LANES, HEAD_DIM)
    seg_mean = _block_ones(LANES, HEAD_DIM) * (1.0 / HEAD_DIM)
    rot = _rotate_half_matrix()
    eye = jnp.eye(BAND_BLOCK, dtype=BF16)
    heads_per_slab = LANES // HEAD_DIM
    gain = lambda a: jnp.tile(a, heads_per_slab).reshape(1, LANES).astype(F32)
    partner = lambda a: gain(jnp.concatenate([a[half:], a[:half]]))

    xc = x.reshape(T, D)
    for i in range(depth):
        zeros = jnp.zeros((DECAY_LORA, RWKV_DIM), F32)
        w2a = jnp.concatenate([jnp.concatenate([rwkv_w2[i], zeros], axis=1),
                               jnp.concatenate([zeros, rwkv_a2[i]], axis=1)], axis=0)

        x1, h2, w_in_bf = _ffn1(xc, rowvec(ffn1_norm[i]), bf(ffn1_w_gate[i]), bf(ffn1_w_up[i]), bf(ffn1_w_down[i]),
                                rowvec(mix_norm[i]), [w_in[i]])
        out_weights = [w_br_rwkv[i], w_br_attn[i], w_out[i], ffn2_w_gate[i], ffn2_w_up[i], ffn2_w_down[i],
                       ple_w_gate[i], ple_w_proj[i]]
        zr, za, gt, wbr, wba, wo, wg2, wu2, wd2, pwg, pwp = _inproj(h2, w_in_bf, out_weights)
        yr = _rwkv(zr.reshape(B, S, RWKV_COLS), rowvec(rwkv_mu[i]), rowvec(rwkv_w0[i]), rowvec(rwkv_a0[i]),
                   bf(w2a), bf(rwkv_g2[i]), rowvec(rwkv_k_k[i]), rowvec(rwkv_k_a[i]), rowvec(rwkv_r_k[i]),
                   rowvec(rwkv_gn_w[i]), rowvec(rwkv_gn_b[i]), seg64, B, S)
        attn_consts = [invf, gain(q_norm[i]), partner(q_norm[i]), gain(k_norm[i]), partner(k_norm[i]),
                       seg_mean, rot, eye, _band_bias(), spread]
        ya = _attn(za.reshape(B, S, ATTN_COLS), pos, attn_consts, B, S)
        consts = [wbr, wba, wo, rowvec(ffn2_norm[i]), wg2, wu2, wd2, rowvec(ple_norm[i]), pwg, pwp]
        xc = _out(x1, gt, yr.reshape(T, RWKV_DIM), ya.reshape(T, ATTN_OUT_DIM), p[i].reshape(T, PLE_DIM), consts)
    return xc.reshape(B, S, D)
```

```python
import functools
import math

import numpy as np
import jax
import jax.numpy as jnp
from jax import lax
from jax.experimental import pallas as pl
from jax.experimental.pallas import tpu as pltpu

F32 = jnp.float32
BF16 = jnp.bfloat16

D_MODEL = 1024
PLE_DIM = 256
HEAD_DIM = 64
RWKV_HEADS = 8
RWKV_DIM = RWKV_HEADS * HEAD_DIM
DECAY_LORA = 64
ICLR_LORA = 64
GATE_LORA = 128
GN_EPS = 64e-5
ATTN_GROUPS = ((128, 1), (512, 4), (2048, 16))
HEADS_PER_GROUP = 4
ATTN_DIM = HEADS_PER_GROUP * len(ATTN_GROUPS) * HEAD_DIM
ATTN_OUT_DIM = HEADS_PER_GROUP * HEAD_DIM
BAND_BLOCK = 128
ROPE_THETA = 10000.0
NEG_INF = -1e30
D_FF = 2816
RMS_EPS = 1e-6
LOG2E = math.log2(math.e)
LN2 = math.log(2.0)
RWKV_COLS = 3 * RWKV_DIM + DECAY_LORA + ICLR_LORA + GATE_LORA
ATTN_COLS = 3 * ATTN_DIM
GATE_COLS = 2 * D_MODEL

LANES = 128
SUBLANES = 8
BF16_ROWS = 16
VMEM_LIMIT_BYTES = 60 * 1024 * 1024

FFN_TM = 512
PROJ_TM = 1024
OUT_TM = 512
RWKV_TS = 512
CHUNK = 64
CHUNKS_PER_ITER = 8
ROW_PARTS = 4
ATTN_ROWS = 256
ATTN_UNITS = 16
FOLD_PITCH = BAND_BLOCK + SUBLANES
POS_PER_ROW = LANES // (HEAD_DIM // 2)


def _dot(a, b):
    return jnp.dot(a, b, preferred_element_type=F32)


def _dot_nt(a, b):
    return lax.dot_general(a, b, (((1,), (1,)), ((), ())), preferred_element_type=F32)


def _dot_tn(a, b):
    return lax.dot_general(a, b, (((0,), (0,)), ((), ())), preferred_element_type=F32)


def _rms(x, gain):
    return x * lax.rsqrt(jnp.mean(x * x, axis=-1, keepdims=True) + RMS_EPS) * gain


def _sigmoid(x):
    return 1.0 / (1.0 + jnp.exp(-x))


def _segsum(x, seg):
    w = seg.shape[0]
    parts = [_dot(x[:, lo:lo + w].astype(BF16), seg) for lo in range(0, x.shape[1], w)]
    return parts[0] if len(parts) == 1 else jnp.concatenate(parts, axis=1)


def _const_spec(arr):
    nd = arr.ndim
    return pl.BlockSpec(arr.shape, lambda *_: (0,) * nd, pipeline_mode=pl.Buffered(1))


def _params(*sem):
    return pltpu.CompilerParams(dimension_semantics=sem, vmem_limit_bytes=VMEM_LIMIT_BYTES)


def _row_parts(n):
    return [slice(i * n // ROW_PARTS, (i + 1) * n // ROW_PARTS) for i in range(ROW_PARTS)]


def _cast_plan(weights, nsteps):
    specs, shapes = [], []
    for w in weights:
        rows, cols = w.shape
        every = 1
        while (rows * every) % nsteps or (rows * every // nsteps) % BF16_ROWS:
            every *= 2
        specs.append(pl.BlockSpec((rows * every // nsteps, cols), lambda i, every=every: (i // every, 0)))
        shapes.append(jax.ShapeDtypeStruct(w.shape, BF16))
    return specs, shapes


def _cast_slabs(src_refs, dst_refs):
    for src, dst in zip(src_refs, dst_refs):
        dst[...] = src[...].astype(BF16)


def _ffn1_kernel(ncast, x_ref, n1_ref, wg_ref, wu_ref, wd_ref, n2_ref, *rest):
    cast_in, (x1_ref, h2_ref), cast_out = rest[:ncast], rest[ncast:ncast + 2], rest[ncast + 2:]
    _cast_slabs(cast_in, cast_out)
    for rows in _row_parts(x_ref.shape[0]):
        x = x_ref[rows, :]
        h = _rms(x, n1_ref[...]).astype(BF16)
        g = _dot(h, wg_ref[...])
        u = _dot(h, wu_ref[...])
        act = (g * _sigmoid(g) * u).astype(BF16)
        x1 = x + 0.5 * _dot(act, wd_ref[...])
        x1_ref[rows, :] = x1
        h2_ref[rows, :] = _rms(x1, n2_ref[...]).astype(BF16)


def _ffn1(x2d, n1, wg, wu, wd, n2, next_weights):
    T = x2d.shape[0]
    nsteps = T // FFN_TM
    row = lambda i: (i, 0)
    cast_specs, cast_shapes = _cast_plan(next_weights, nsteps)
    return pl.pallas_call(
        functools.partial(_ffn1_kernel, len(next_weights)),
        grid=(nsteps,),
        in_specs=[pl.BlockSpec((FFN_TM, D_MODEL), row), _const_spec(n1), _const_spec(wg),
                  _const_spec(wu), _const_spec(wd), _const_spec(n2)] + cast_specs,
        out_specs=[pl.BlockSpec((FFN_TM, D_MODEL), row), pl.BlockSpec((FFN_TM, D_MODEL), row)] + cast_specs,
        out_shape=[jax.ShapeDtypeStruct((T, D_MODEL), F32), jax.ShapeDtypeStruct((T, D_MODEL), BF16)]
                  + cast_shapes,
        compiler_params=_params("arbitrary"),
        name="ffn1",
    )(x2d, n1, wg, wu, wd, n2, *next_weights)


def _inproj_kernel(ncast, h_ref, w_ref, *rest):
    cast_in, (zr_ref, za_ref, gt_ref), cast_out = rest[:ncast], rest[ncast:ncast + 3], rest[ncast + 3:]
    _cast_slabs(cast_in, cast_out)
    for rows in _row_parts(h_ref.shape[0]):
        h = h_ref[rows, :]
        zr_ref[rows, :] = _dot(h, w_ref[:, :RWKV_COLS])
        za_ref[rows, :] = _dot(h, w_ref[:, RWKV_COLS:RWKV_COLS + ATTN_COLS]).astype(BF16)
        gt_ref[rows, :] = _sigmoid(_dot(h, w_ref[:, RWKV_COLS + ATTN_COLS:])).astype(BF16)


def _inproj(h2, w, next_weights):
    T = h2.shape[0]
    nsteps = T // PROJ_TM
    row = lambda i: (i, 0)
    cast_specs, cast_shapes = _cast_plan(next_weights, nsteps)
    return pl.pallas_call(
        functools.partial(_inproj_kernel, len(next_weights)),
        grid=(nsteps,),
        in_specs=[pl.BlockSpec((PROJ_TM, D_MODEL), row), _const_spec(w)] + cast_specs,
        out_specs=[pl.BlockSpec((PROJ_TM, RWKV_COLS), row), pl.BlockSpec((PROJ_TM, ATTN_COLS), row),
                   pl.BlockSpec((PROJ_TM, GATE_COLS), row)] + cast_specs,
        out_shape=[jax.ShapeDtypeStruct((T, RWKV_COLS), F32), jax.ShapeDtypeStruct((T, ATTN_COLS), BF16),
                   jax.ShapeDtypeStruct((T, GATE_COLS), BF16)] + cast_shapes,
        compiler_params=_params("arbitrary"),
        name="inproj",
    )(h2, w, *next_weights)


def _scan_chunks(row_list, h_ref, r_s, k_s, v_s, na_s, b_s, lw_s, cum_s, y_s):
    C = CHUNK
    C2 = 2 * C
    pairs = range(RWKV_HEADS // 2)
    rows_of = [rows for rows in row_list for _ in pairs]
    lanes = [slice(pr * LANES, (pr + 1) * LANES) for _ in row_list for pr in pairs]
    first = lax.broadcasted_iota(jnp.int32, (C, LANES), 1) < HEAD_DIM
    ri = lax.broadcasted_iota(jnp.int32, (C2, C2), 0)
    ci = lax.broadcasted_iota(jnp.int32, (C2, C2), 1)
    strict = ci < ri
    incl = ci <= ri
    diag = ci == ri
    eye = jnp.where(diag, 1.0, 0.0)

    def stack(x):
        return jnp.concatenate([jnp.where(first, x, 0.0), jnp.where(first, 0.0, x)], axis=0)

    def cat(parts, axis):
        return jnp.concatenate(parts, axis=axis)

    def load(ref):
        return [ref[rows, ln] for rows, ln in zip(rows_of, lanes)]

    lw = load(lw_s)
    cum = load(cum_s)
    p_inc = [jnp.exp(x) for x in cum]
    p_exc = [jnp.exp(x - y) for x, y in zip(cum, lw)]
    p_inv = [jnp.exp(-x) for x in cum]
    p_end = [x[C - 1:C, :] for x in p_inc]
    to_end = [x * y for x, y in zip(p_inv, p_end)]

    a_st = [stack(x * p) for x, p in zip(load(na_s), p_exc)]
    r_st = [stack(x * p) for x, p in zip(load(r_s), p_inc)]
    v_bf = [stack(x).astype(BF16) for x in load(v_s)]
    xr = [cat([a, r], 0).astype(BF16) for a, r in zip(a_st, r_st)]
    b = load(b_s)
    k = load(k_s)
    y_g = [cat([stack(bb * p), stack(kk * p)], 0).astype(BF16) for bb, kk, p in zip(b, k, p_inv)]
    y_h = [cat([stack(bb * p), stack(kk * p)], 0).astype(BF16) for bb, kk, p in zip(b, k, to_end)]

    g = [_dot_nt(x, y) for x, y in zip(xr, y_g)]
    l_ab = [jnp.where(strict, x[:C2, :C2], 0.0) for x in g]
    lmk = [cat([jnp.where(strict, x[:C2, C2:], 0.0), jnp.where(incl, x[C2:, C2:], 0.0)], 0).astype(BF16)
           for x in g]
    m_rb = [jnp.where(incl, x[C2:, :C2], 0.0).astype(BF16) for x in g]

    l_bf = [x.astype(BF16) for x in l_ab]
    pw = [_dot(x, x) for x in l_bf]
    acc = [eye + x for x in l_ab]
    for _ in range(1, int(math.log2(C)) - 1):
        ps = [_dot(p.astype(BF16), cat([s, p], 1).astype(BF16)) for p, s in zip(pw, acc)]
        acc = [s + x[:, :C2] for s, x in zip(acc, ps)]
        pw = [x[:, C2:] for x in ps]
    acc = [s + _dot(p.astype(BF16), s.astype(BF16)) for p, s in zip(pw, acc)]

    lm = [_dot(x, vv) for x, vv in zip(lmk, v_bf)]
    av = [_dot(t.astype(BF16), cat([a, x[:C2]], 1).astype(BF16)).astype(BF16)
          for t, a, x in zip(acc, a_st, lm)]
    ry = [_dot(m, x) + cat([r, y[C2:]], 1) for m, x, r, y in zip(m_rb, av, r_st, lm)]
    zero = jnp.zeros((C2, C2), BF16)
    md = [_dot_tn(yh, cat([x, cat([zero, vv], 1)], 0)) for yh, x, vv in zip(y_h, av, v_bf)]
    rmp = [cat([y[:, :C2], jnp.where(diag, jnp.broadcast_to(pe, (C2, C2)), 0.0) + x[:, :C2]], 0).astype(BF16)
           for y, pe, x in zip(ry, p_end, md)]

    h = [h_ref[pr] for pr in pairs]
    for ci_, rows in enumerate(row_list):
        base = ci_ * len(pairs)
        rm = [_dot(rmp[base + pr], h[pr].astype(BF16)) for pr in pairs]
        h = [rm[pr][C2:] + md[base + pr][:, C2:] for pr in pairs]
        for pr in pairs:
            y_st = rm[pr][:C2] + ry[base + pr][:, C2:]
            y_s[rows, lanes[pr]] = y_st[:C] + y_st[C:]
    for pr in pairs:
        h_ref[pr] = h[pr]


def _rwkv_kernel(z_ref, zh_ref, mu_ref, w0_ref, a0_ref, w2a_ref, g2_ref, kk_ref, ka_ref, rk_ref,
                 gnw_ref, gnb_ref, seg_ref, tril_ref, y_ref,
                 h_ref, r_s, k_s, v_s, na_s, b_s, lw_s, cum_s, g_s, y_s):
    t = pl.program_id(1)
    TS = RWKV_TS
    D = RWKV_DIM

    @pl.when(t == 0)
    def _():
        h_ref[...] = jnp.zeros_like(h_ref)

    has_prev = jnp.where(t > 0, 1.0, 0.0).astype(F32)

    def shifted(lo, hi):
        zc = z_ref[:, lo:hi]
        prev = zh_ref[SUBLANES - 1:SUBLANES, lo:hi] * has_prev
        rolled = pltpu.roll(zc, 1, 0)
        row = lax.broadcasted_iota(jnp.int32, (SUBLANES, hi - lo), 0)
        zp = jnp.concatenate([jnp.where(row == 0, prev, rolled[:SUBLANES]), rolled[SUBLANES:]], axis=0)
        return zc + (zp - zc) * mu_ref[:, lo:hi]

    seg = seg_ref[...]
    r = shifted(0, D)
    k = shifted(D, 2 * D)
    v = shifted(2 * D, 3 * D)
    wa = shifted(3 * D, 3 * D + DECAY_LORA + ICLR_LORA)
    gd = shifted(3 * D + DECAY_LORA + ICLR_LORA, RWKV_COLS)

    lane = lax.broadcasted_iota(jnp.int32, wa.shape, 1)
    lora_in = jnp.where(lane < DECAY_LORA, jnp.tanh(wa), wa).astype(BF16)
    lora = _dot(lora_in, w2a_ref[...])
    lw = -math.exp(-0.5) * _sigmoid(w0_ref[...] + lora[:, :D])
    lw_s[...] = lw
    tril = tril_ref[...]
    hi = lw.astype(BF16)
    lo = (lw - hi.astype(F32)).astype(BF16)
    span = tril.shape[0]
    for r0 in range(0, TS, span):
        cum_s[r0:r0 + span, :] = _dot(tril, hi[r0:r0 + span]) + _dot(tril, lo[r0:r0 + span])
    a = _sigmoid(a0_ref[...] + lora[:, D:])
    g_s[...] = _dot(_sigmoid(gd).astype(BF16), g2_ref[...])

    kk = k * kk_ref[...]
    kk = kk * lax.rsqrt(jnp.maximum(_segsum(kk * kk, seg), 1e-24))
    k2 = k * (1.0 + (a - 1.0) * ka_ref[...])
    r_s[...] = r
    k_s[...] = k2
    v_s[...] = v
    na_s[...] = -kk
    b_s[...] = kk * a

    def chunk_body(c, carry):
        row_list = [pl.ds(pl.multiple_of((c * CHUNKS_PER_ITER + i) * CHUNK, CHUNK), CHUNK)
                    for i in range(CHUNKS_PER_ITER)]
        _scan_chunks(row_list, h_ref, r_s, k_s, v_s, na_s, b_s, lw_s, cum_s, y_s)
        return carry

    lax.fori_loop(0, TS // (CHUNK * CHUNKS_PER_ITER), chunk_body, 0)

    y = y_s[...]
    inv_n = 1.0 / HEAD_DIM
    mean = _segsum(y, seg) * inv_n
    yc = y - mean
    var = _segsum(yc * yc, seg) * inv_n
    yn = yc * lax.rsqrt(var + GN_EPS) * gnw_ref[...] + gnb_ref[...]
    bonus = _segsum(r_s[...] * k_s[...] * rk_ref[...], seg) * v_s[...]
    y_ref[...] = ((yn + bonus) * g_s[...]).astype(BF16)


def _rwkv(zr, mu, w0, a0, w2a, g2, k_k, k_a, r_k, gn_w, gn_b, seg, B, S):
    TS = RWKV_TS
    t_idx = np.arange(2 * CHUNK)
    tril = jnp.asarray((t_idx[:, None] // CHUNK == t_idx[None, :] // CHUNK)
                       & (t_idx[None, :] <= t_idx[:, None]), BF16)
    consts = [mu, w0, a0, w2a, g2, k_k, k_a, r_k, gn_w, gn_b, seg, tril]
    halo_blocks = TS // SUBLANES
    stage = pltpu.VMEM((TS, RWKV_DIM), F32)
    return pl.pallas_call(
        _rwkv_kernel,
        grid=(B, S // TS),
        in_specs=[pl.BlockSpec((None, TS, RWKV_COLS), lambda b, t: (b, t, 0)),
                  pl.BlockSpec((None, SUBLANES, RWKV_COLS),
                               lambda b, t: (b, jnp.maximum(t * halo_blocks - 1, 0), 0))]
                 + [_const_spec(c) for c in consts],
        out_specs=pl.BlockSpec((None, TS, RWKV_DIM), lambda b, t: (b, t, 0)),
        out_shape=jax.ShapeDtypeStruct((B, S, RWKV_DIM), BF16),
        scratch_shapes=[pltpu.VMEM((RWKV_HEADS // 2, 2 * CHUNK, 2 * CHUNK), F32)] + [stage] * 9,
        compiler_params=_params("parallel", "arbitrary"),
        name="rwkv",
    )(zr, zr, *consts)


Q_SLAB, K_SLAB, V_SLAB, N_SLABS = 0, 4, 6, 10


def _fold_rows(tile, sub, dil):
    return pl.ds(tile * (ATTN_ROWS // dil) + sub, dil, stride=FOLD_PITCH)


def _attn_group(gi, dil, S, x_s, o_s, l_s, eye_ref, bias_ref):
    L = S // dil
    nb = L // BAND_BLOCK
    Q = BAND_BLOCK
    heads = range(HEADS_PER_GROUP)
    lane2 = lax.broadcasted_iota(jnp.int32, (Q, 2 * LANES), 1)
    hmask = [lane2 // HEAD_DIM == j for j in heads]
    eye = eye_ref[...]

    def load(slab, rows):
        return x_s[slab, rows, :].astype(BF16)

    def step(it, carry):
        cur, q, k_aug, v = [], [], [], []
        for n in range(ATTN_UNITS):
            u = it * ATTN_UNITS + n
            if nb > 1:
                res, blk = u // nb, u % nb
                rows = pl.ds(res + blk * (Q * dil), Q, stride=dil)
                prev = pl.ds(res + jnp.maximum(blk - 1, 0) * (Q * dil), Q, stride=dil)
                both = lambda slab, prev=prev, rows=rows: jnp.concatenate([load(slab, prev), load(slab, rows)],
                                                                          axis=0)
                bias = bias_ref[jnp.minimum(blk, 1)]
            else:
                rows = pl.ds(pl.multiple_of(u * FOLD_PITCH, SUBLANES), Q)
                both = lambda slab, rows=rows: load(slab, rows)
                bias = bias_ref[1, Q:, :]
            cur.append(rows)
            q.append([jnp.concatenate([load(Q_SLAB + j, rows), eye], axis=1) for j in heads])
            k_aug.append([jnp.concatenate([both(K_SLAB + hp), bias], axis=1) for hp in range(2)])
            v.append([both(V_SLAB + j) for j in heads])
        chains = [(n, j) for n in range(ATTN_UNITS) for j in heads]
        s = [_dot_nt(q[n][j], k_aug[n][j // 2]) for n, j in chains]
        m = [jnp.max(x, axis=-1, keepdims=True) for x in s]
        e = [jnp.exp2(x - y) for x, y in zip(s, m)]
        l = [jnp.sum(x, axis=-1, keepdims=True) for x in e]
        p = [(x * (1.0 / y)).astype(BF16) for x, y in zip(e, l)]
        for n in range(ATTN_UNITS):
            c = n * HEADS_PER_GROUP
            lse = jnp.broadcast_to(m[c] * LN2 + jnp.log(l[c]), (Q, 2 * LANES))
            for j in heads[1:]:
                lse = jnp.where(hmask[j], m[c + j] * LN2 + jnp.log(l[c + j]), lse)
            for hp in range(2):
                pp = jnp.concatenate([p[c + 2 * hp], p[c + 2 * hp + 1]], axis=1)
                vv = jnp.concatenate([v[n][2 * hp], v[n][2 * hp + 1]], axis=0)
                o_s[gi, hp, cur[n], :] = _dot(pp, vv)
                l_s[gi, hp, cur[n], :] = lse[:, hp * LANES:(hp + 1) * LANES]
        return carry

    lax.fori_loop(0, dil * nb // ATTN_UNITS, step, 0)


def _attn_kernel(q_ref, k_ref, v_ref, pos_ref, invf_ref, qg_ref, qgp_ref, kg_ref, kgp_ref, seg_ref, rot_ref,
                 eye_ref, bias_ref, spread_ref, out_ref, x_s, o_s, l_s, cs_s):
    g = pl.program_id(1)
    S = q_ref.shape[0]
    R = ATTN_ROWS
    ng = len(ATTN_GROUPS)
    fold_dil = ATTN_GROUPS[-1][1]

    @pl.when(g == 0)
    def _():
        rc = R

        def body(i, carry):
            ang = pos_ref[pl.ds(pl.multiple_of(i * rc, rc), rc), :].astype(F32) * invf_ref[...]
            for tbl, fn in enumerate((jnp.cos, jnp.sin)):
                val = fn(ang)
                hi = val.astype(BF16)
                rest = val - hi.astype(F32)
                mid = rest.astype(BF16)
                lo = (rest - mid.astype(F32)).astype(BF16)
                q_gain, k_gain = ((qg_ref, kg_ref), (qgp_ref, kgp_ref))[tbl]
                for sub in range(POS_PER_ROW):
                    spread = spread_ref[sub]
                    full = _dot(hi, spread) + _dot(mid, spread) + _dot(lo, spread)
                    rows = pl.ds(i * (rc * POS_PER_ROW) + sub, rc, stride=POS_PER_ROW)
                    cs_s[tbl, rows, :] = full * (q_gain[...] * (HEAD_DIM ** -0.5 * LOG2E))
                    cs_s[2 + tbl, rows, :] = full * k_gain[...]
            return carry
        lax.fori_loop(0, S // (rc * POS_PER_ROW), body, 0)

    seg = seg_ref[...]
    rot = rot_ref[...]
    first = lax.broadcasted_iota(jnp.int32, (R, LANES), 1) < HEAD_DIM

    def prep(folded, i, carry):
        rows = pl.ds(pl.multiple_of(i * R, R), R)

        def store(slab, val):
            if folded:
                for sub in range(R // fold_dil):
                    x_s[slab, _fold_rows(i, sub, fold_dil), :] = val[sub * fold_dil:(sub + 1) * fold_dil]
            else:
                x_s[slab, rows, :] = val

        def store_per_head(slab, val):
            store(slab, jnp.where(first, val, 0.0))
            store(slab + 1, jnp.where(first, 0.0, val))

        def normed(src, hf, tables):
            x = src[rows, hf * LANES:(hf + 1) * LANES]
            xf = x.astype(F32)
            inv = lax.rsqrt(_segsum(xf * xf, seg) + RMS_EPS)
            xr = _dot(x, rot)
            return (xf * cs_s[tables, rows, :] + xr * cs_s[tables + 1, rows, :]) * inv

        for hf in range(2):
            store_per_head(Q_SLAB + 2 * hf, normed(q_ref, hf, 0))
            store(K_SLAB + hf, normed(k_ref, hf, 2))
            store_per_head(V_SLAB + 2 * hf, v_ref[rows, hf * LANES:(hf + 1) * LANES].astype(F32))
        return carry

    @pl.when(g < ng - 1)
    def _():
        lax.fori_loop(0, S // R, functools.partial(prep, False), 0)

    @pl.when(g == ng - 1)
    def _():
        lax.fori_loop(0, S // R, functools.partial(prep, True), 0)

    for gi, (_, dil) in enumerate(ATTN_GROUPS):
        pl.when(g == gi)(functools.partial(_attn_group, gi, dil, S, x_s, o_s, l_s, eye_ref, bias_ref))

    @pl.when(g == ng - 1)
    def _():
        def body(i, carry):
            rows = pl.ds(pl.multiple_of(i * R, R), R)

            def fetch(ref, gi, slab):
                if gi < ng - 1:
                    return ref[gi, slab, rows, :]
                return jnp.concatenate([ref[gi, slab, _fold_rows(i, sub, fold_dil), :]
                                        for sub in range(R // fold_dil)], axis=0)

            for slab in range(2):
                ls = [fetch(l_s, gi, slab) for gi in range(ng)]
                m = jnp.maximum(jnp.maximum(ls[0], ls[1]), ls[2])
                ws = [jnp.exp(x - m) for x in ls]
                num = sum(w * fetch(o_s, gi, slab) for gi, w in enumerate(ws))
                out_ref[rows, slab * LANES:(slab + 1) * LANES] = (num / (ws[0] + ws[1] + ws[2])).astype(BF16)
            return carry
        lax.fori_loop(0, S // R, body, 0)


def _attn(za, pos, consts, B, S):
    ng = len(ATTN_GROUPS)
    gw = ATTN_OUT_DIM
    rows = (S // BAND_BLOCK) * FOLD_PITCH
    return pl.pallas_call(
        _attn_kernel,
        grid=(B, ng),
        in_specs=[pl.BlockSpec((None, S, gw), lambda b, g: (b, 0, g)),
                  pl.BlockSpec((None, S, gw), lambda b, g: (b, 0, ng + g)),
                  pl.BlockSpec((None, S, gw), lambda b, g: (b, 0, 2 * ng + g)),
                  pl.BlockSpec((None, S // POS_PER_ROW, LANES), lambda b, g: (b, 0, 0))]
                 + [_const_spec(c) for c in consts],
        out_specs=pl.BlockSpec((None, S, ATTN_OUT_DIM), lambda b, g: (b, 0, 0)),
        out_shape=jax.ShapeDtypeStruct((B, S, ATTN_OUT_DIM), BF16),
        scratch_shapes=[pltpu.VMEM((N_SLABS, rows, LANES), F32), pltpu.VMEM((ng, 2, rows, LANES), F32),
                        pltpu.VMEM((ng, 2, rows, LANES), F32), pltpu.VMEM((4, S, LANES), F32)],
        compiler_params=_params("parallel", "arbitrary"),
        name="attn",
    )(za, za, za, pos, *consts)


def _out_kernel(x1_ref, gt_ref, yr_ref, ya_ref, p_ref, wbr_ref, wba_ref, wo_ref, n2_ref, wg_ref, wu_ref,
                wd_ref, pn_ref, pwg_ref, pwp_ref, out_ref):
    br = _dot(yr_ref[...], wbr_ref[...])
    ba = _dot(ya_ref[...], wba_ref[...])
    merged = gt_ref[:, :D_MODEL].astype(F32) * br + gt_ref[:, D_MODEL:].astype(F32) * ba
    x2 = x1_ref[...] + _dot(merged.astype(BF16), wo_ref[...])
    h = _rms(x2, n2_ref[...]).astype(BF16)
    g = _dot(h, wg_ref[...])
    u = _dot(h, wu_ref[...])
    act = (g * _sigmoid(g) * u).astype(BF16)
    x3 = x2 + 0.5 * _dot(act, wd_ref[...])
    hp = _rms(x3, pn_ref[...]).astype(BF16)
    gate = _sigmoid(_dot(hp, pwg_ref[...]))
    out_ref[...] = x3 + gate * _dot(p_ref[...].astype(BF16), pwp_ref[...])


def _out(x1, gt, yr, ya, p2d, consts):
    T = x1.shape[0]
    TM = OUT_TM
    row = lambda i: (i, 0)
    return pl.pallas_call(
        _out_kernel,
        grid=(T // TM,),
        in_specs=[pl.BlockSpec((TM, D_MODEL), row), pl.BlockSpec((TM, GATE_COLS), row),
                  pl.BlockSpec((TM, RWKV_DIM), row), pl.BlockSpec((TM, ATTN_OUT_DIM), row),
                  pl.BlockSpec((TM, PLE_DIM), row)] + [_const_spec(c) for c in consts],
        out_specs=pl.BlockSpec((TM, D_MODEL), row),
        out_shape=jax.ShapeDtypeStruct((T, D_MODEL), F32),
        compiler_params=_params("parallel"),
        name="merge_ffn2",
    )(x1, gt, yr, ya, p2d, *consts)


def _rotate_half_matrix():
    half = HEAD_DIM // 2
    m = np.zeros((LANES, LANES), np.float32)
    for lane in range(LANES):
        if lane % HEAD_DIM < half:
            m[lane + half, lane] = -1.0
        else:
            m[lane - half, lane] = 1.0
    return jnp.asarray(m, BF16)


def _band_bias():
    kq = np.arange(BAND_BLOCK)[:, None]
    qq = np.arange(BAND_BLOCK)[None, :]
    cur = np.where(kq <= qq, 0.0, NEG_INF)
    prev = np.where(kq >= qq, 0.0, NEG_INF)
    none = np.full_like(prev, NEG_INF)
    return jnp.asarray(np.stack([np.concatenate([none, cur]), np.concatenate([prev, cur])]), BF16)


def _block_ones(n, blk):
    idx = np.arange(n) // blk
    return jnp.asarray(idx[:, None] == idx[None, :], BF16)


def kernel(x, p, positions, ffn1_norm, ffn1_w_gate, ffn1_w_up, ffn1_w_down, mix_norm, w_in, rwkv_mu, rwkv_w0, rwkv_w2, rwkv_a0, rwkv_a2, rwkv_g2, rwkv_k_k, rwkv_k_a, rwkv_r_k, rwkv_gn_w, rwkv_gn_b, q_norm, k_norm, w_br_rwkv, w_br_attn, w_out, ffn2_norm, ffn2_w_gate, ffn2_w_up, ffn2_w_down, ple_norm, ple_w_gate, ple_w_proj):
    B, S, D = x.shape
    depth = p.shape[0]
    T = B * S
    half = HEAD_DIM // 2
    bf = lambda w: w.astype(BF16)
    rowvec = lambda a: a.reshape(1, -1).astype(F32)

    inv_freq = 1.0 / (ROPE_THETA ** (jnp.arange(0, HEAD_DIM, 2, dtype=F32) / HEAD_DIM))
    invf = jnp.tile(inv_freq, HEADS_PER_GROUP).reshape(1, LANES)
    pos = jnp.repeat(positions.reshape(B, S // POS_PER_ROW, POS_PER_ROW), half, axis=2)
    lane = np.arange(LANES)
    spread = jnp.asarray(np.stack([lane[:, None] == sub * half + lane[None, :] % half
                                   for sub in range(POS_PER_ROW)]), BF16)
    seg64 = _block_ones(2 * LANES, HEAD_DIM)
    seg_mean = _block_ones(LANES, HEAD_DIM) * (1.0 / HEAD_DIM)
    rot = _rotate_half_matrix()
    eye = jnp.eye(BAND_BLOCK, dtype=BF16)
    heads_per_slab = LANES // HEAD_DIM
    gain = lambda a: jnp.tile(a, heads_per_slab).reshape(1, LANES).astype(F32)
    partner = lambda a: gain(jnp.concatenate([a[half:], a[:half]]))

    xc = x.reshape(T, D)
    for i in range(depth):
        zeros = jnp.zeros((DECAY_LORA, RWKV_DIM), F32)
        w2a = jnp.concatenate([jnp.concatenate([rwkv_w2[i], zeros], axis=1),
                               jnp.concatenate([zeros, rwkv_a2[i]], axis=1)], axis=0)

        x1, h2, w_in_bf = _ffn1(xc, rowvec(ffn1_norm[i]), bf(ffn1_w_gate[i]), bf(ffn1_w_up[i]), bf(ffn1_w_down[i]),
                                rowvec(mix_norm[i]), [w_in[i]])
        out_weights = [w_br_rwkv[i], w_br_attn[i], w_out[i], ffn2_w_gate[i], ffn2_w_up[i], ffn2_w_down[i],
                       ple_w_gate[i], ple_w_proj[i]]
        zr, za, gt, wbr, wba, wo, wg2, wu2, wd2, pwg, pwp = _inproj(h2, w_in_bf, out_weights)
        yr = _rwkv(zr.reshape(B, S, RWKV_COLS), rowvec(rwkv_mu[i]), rowvec(rwkv_w0[i]), rowvec(rwkv_a0[i]),
                   bf(w2a), bf(rwkv_g2[i]), rowvec(rwkv_k_k[i]), rowvec(rwkv_k_a[i]), rowvec(rwkv_r_k[i]),
                   rowvec(rwkv_gn_w[i]), rowvec(rwkv_gn_b[i]), seg64, B, S)
        attn_consts = [invf, gain(q_norm[i]), partner(q_norm[i]), gain(k_norm[i]), partner(k_norm[i]),
                       seg_mean, rot, eye, _band_bias(), spread]
        ya = _attn(za.reshape(B, S, ATTN_COLS), pos, attn_consts, B, S)
        consts = [wbr, wba, wo, rowvec(ffn2_norm[i]), wg2, wu2, wd2, rowvec(ple_norm[i]), pwg, pwp]
        xc = _out(x1, gt, yr.reshape(T, RWKV_DIM), ya.reshape(T, ATTN_OUT_DIM), p[i].reshape(T, PLE_DIM), consts)
    return xc.reshape(B, S, D)
```

```python
import functools
import math

import numpy as np
import jax
import jax.numpy as jnp
from jax import lax
from jax.experimental import pallas as pl
from jax.experimental.pallas import tpu as pltpu

F32 = jnp.float32
BF16 = jnp.bfloat16

D_MODEL = 1024
PLE_DIM = 256
HEAD_DIM = 64
RWKV_HEADS = 8
RWKV_DIM = RWKV_HEADS * HEAD_DIM
DECAY_LORA = 64
ICLR_LORA = 64
GATE_LORA = 128
GN_EPS = 64e-5
ATTN_GROUPS = ((128, 1), (512, 4), (2048, 16))
HEADS_PER_GROUP = 4
ATTN_DIM = HEADS_PER_GROUP * len(ATTN_GROUPS) * HEAD_DIM
ATTN_OUT_DIM = HEADS_PER_GROUP * HEAD_DIM
BAND_BLOCK = 128
ROPE_THETA = 10000.0
NEG_INF = -1e30
D_FF = 2816
RMS_EPS = 1e-6
LOG2E = math.log2(math.e)
LN2 = math.log(2.0)
RWKV_COLS = 3 * RWKV_DIM + DECAY_LORA + ICLR_LORA + GATE_LORA
ATTN_COLS = 3 * ATTN_DIM
GATE_COLS = 2 * D_MODEL

LANES = 128
SUBLANES = 8
BF16_ROWS = 16
VMEM_LIMIT_BYTES = 60 * 1024 * 1024

FFN_TM = 512
PROJ_TM = 1024
OUT_TM = 512
RWKV_TS = 512
CHUNK = 64
ROW_PARTS = 4
ATTN_ROWS = 256
FOLD_PITCH = BAND_BLOCK + SUBLANES
POS_PER_ROW = LANES // (HEAD_DIM // 2)


def _dot(a, b):
    return jnp.dot(a, b, preferred_element_type=F32)


def _dot_nt(a, b):
    return lax.dot_general(a, b, (((1,), (1,)), ((), ())), preferred_element_type=F32)


def _dot_tn(a, b):
    return lax.dot_general(a, b, (((0,), (0,)), ((), ())), preferred_element_type=F32)


def _rms(x, gain):
    return x * lax.rsqrt(jnp.mean(x * x, axis=-1, keepdims=True) + RMS_EPS) * gain


def _sigmoid(x):
    return 1.0 / (1.0 + jnp.exp(-x))


def _segsum(x, seg):
    w = seg.shape[0]
    parts = [_dot(x[:, lo:lo + w].astype(BF16), seg) for lo in range(0, x.shape[1], w)]
    return parts[0] if len(parts) == 1 else jnp.concatenate(parts, axis=1)


def _const_spec(arr):
    nd = arr.ndim
    return pl.BlockSpec(arr.shape, lambda *_: (0,) * nd, pipeline_mode=pl.Buffered(1))


def _params(*sem):
    return pltpu.CompilerParams(dimension_semantics=sem, vmem_limit_bytes=VMEM_LIMIT_BYTES)


def _row_parts(n):
    return [slice(i * n // ROW_PARTS, (i + 1) * n // ROW_PARTS) for i in range(ROW_PARTS)]


def _cast_plan(weights, nsteps):
    specs, shapes = [], []
    for w in weights:
        rows, cols = w.shape
        every = 1
        while (rows * every) % nsteps or (rows * every // nsteps) % BF16_ROWS:
            every *= 2
        specs.append(pl.BlockSpec((rows * every // nsteps, cols), lambda i, every=every: (i // every, 0)))
        shapes.append(jax.ShapeDtypeStruct(w.shape, BF16))
    return specs, shapes


def _cast_slabs(src_refs, dst_refs):
    for src, dst in zip(src_refs, dst_refs):
        dst[...] = src[...].astype(BF16)


def _ffn1_kernel(ncast, x_ref, n1_ref, wg_ref, wu_ref, wd_ref, n2_ref, *rest):
    cast_in, (x1_ref, h2_ref), cast_out = rest[:ncast], rest[ncast:ncast + 2], rest[ncast + 2:]
    _cast_slabs(cast_in, cast_out)
    for rows in _row_parts(x_ref.shape[0]):
        x = x_ref[rows, :]
        h = _rms(x, n1_ref[...]).astype(BF16)
        g = _dot(h, wg_ref[...])
        u = _dot(h, wu_ref[...])
        act = (g * _sigmoid(g) * u).astype(BF16)
        x1 = x + 0.5 * _dot(act, wd_ref[...])
        x1_ref[rows, :] = x1
        h2_ref[rows, :] = _rms(x1, n2_ref[...]).astype(BF16)


def _ffn1(x2d, n1, wg, wu, wd, n2, next_weights):
    T = x2d.shape[0]
    nsteps = T // FFN_TM
    row = lambda i: (i, 0)
    cast_specs, cast_shapes = _cast_plan(next_weights, nsteps)
    return pl.pallas_call(
        functools.partial(_ffn1_kernel, len(next_weights)),
        grid=(nsteps,),
        in_specs=[pl.BlockSpec((FFN_TM, D_MODEL), row), _const_spec(n1), _const_spec(wg),
                  _const_spec(wu), _const_spec(wd), _const_spec(n2)] + cast_specs,
        out_specs=[pl.BlockSpec((FFN_TM, D_MODEL), row), pl.BlockSpec((FFN_TM, D_MODEL), row)] + cast_specs,
        out_shape=[jax.ShapeDtypeStruct((T, D_MODEL), F32), jax.ShapeDtypeStruct((T, D_MODEL), BF16)]
                  + cast_shapes,
        compiler_params=_params("arbitrary"),
        name="ffn1",
    )(x2d, n1, wg, wu, wd, n2, *next_weights)


def _inproj_kernel(ncast, h_ref, w_ref, *rest):
    cast_in, (zr_ref, za_ref, gt_ref), cast_out = rest[:ncast], rest[ncast:ncast + 3], rest[ncast + 3:]
    _cast_slabs(cast_in, cast_out)
    for rows in _row_parts(h_ref.shape[0]):
        h = h_ref[rows, :]
        zr_ref[rows, :] = _dot(h, w_ref[:, :RWKV_COLS])
        za_ref[rows, :] = _dot(h, w_ref[:, RWKV_COLS:RWKV_COLS + ATTN_COLS]).astype(BF16)
        gt_ref[rows, :] = _sigmoid(_dot(h, w_ref[:, RWKV_COLS + ATTN_COLS:])).astype(BF16)


def _inproj(h2, w, next_weights):
    T = h2.shape[0]
    nsteps = T // PROJ_TM
    row = lambda i: (i, 0)
    cast_specs, cast_shapes = _cast_plan(next_weights, nsteps)
    return pl.pallas_call(
        functools.partial(_inproj_kernel, len(next_weights)),
        grid=(nsteps,),
        in_specs=[pl.BlockSpec((PROJ_TM, D_MODEL), row), _const_spec(w)] + cast_specs,
        out_specs=[pl.BlockSpec((PROJ_TM, RWKV_COLS), row), pl.BlockSpec((PROJ_TM, ATTN_COLS), row),
                   pl.BlockSpec((PROJ_TM, GATE_COLS), row)] + cast_specs,
        out_shape=[jax.ShapeDtypeStruct((T, RWKV_COLS), F32), jax.ShapeDtypeStruct((T, ATTN_COLS), BF16),
                   jax.ShapeDtypeStruct((T, GATE_COLS), BF16)] + cast_shapes,
        compiler_params=_params("arbitrary"),
        name="inproj",
    )(h2, w, *next_weights)


def _scan_chunks(row_list, h_ref, r_s, k_s, v_s, na_s, b_s, lw_s, cum_s, y_s):
    C = CHUNK
    C2 = 2 * C
    pairs = range(RWKV_HEADS // 2)
    rows_of = [rows for rows in row_list for _ in pairs]
    lanes = [slice(pr * LANES, (pr + 1) * LANES) for _ in row_list for pr in pairs]
    first = lax.broadcasted_iota(jnp.int32, (C, LANES), 1) < HEAD_DIM
    ri = lax.broadcasted_iota(jnp.int32, (C2, C2), 0)
    ci = lax.broadcasted_iota(jnp.int32, (C2, C2), 1)
    strict = ci < ri
    incl = ci <= ri
    diag = ci == ri
    eye = jnp.where(diag, 1.0, 0.0)

    def stack(x):
        return jnp.concatenate([jnp.where(first, x, 0.0), jnp.where(first, 0.0, x)], axis=0)

    def cat(parts, axis):
        return jnp.concatenate(parts, axis=axis)

    def load(ref):
        return [ref[rows, ln] for rows, ln in zip(rows_of, lanes)]

    lw = load(lw_s)
    cum = load(cum_s)
    p_inc = [jnp.exp(x) for x in cum]
    p_exc = [jnp.exp(x - y) for x, y in zip(cum, lw)]
    p_inv = [jnp.exp(-x) for x in cum]
    p_end = [x[C - 1:C, :] for x in p_inc]
    to_end = [x * y for x, y in zip(p_inv, p_end)]

    a_st = [stack(x * p) for x, p in zip(load(na_s), p_exc)]
    r_st = [stack(x * p) for x, p in zip(load(r_s), p_inc)]
    v_bf = [stack(x).astype(BF16) for x in load(v_s)]
    xr = [cat([a, r], 0).astype(BF16) for a, r in zip(a_st, r_st)]
    b = load(b_s)
    k = load(k_s)
    y_g = [cat([stack(bb * p), stack(kk * p)], 0).astype(BF16) for bb, kk, p in zip(b, k, p_inv)]
    y_h = [cat([stack(bb * p), stack(kk * p)], 0).astype(BF16) for bb, kk, p in zip(b, k, to_end)]

    g = [_dot_nt(x, y) for x, y in zip(xr, y_g)]
    l_ab = [jnp.where(strict, x[:C2, :C2], 0.0) for x in g]
    lmk = [cat([jnp.where(strict, x[:C2, C2:], 0.0), jnp.where(incl, x[C2:, C2:], 0.0)], 0).astype(BF16)
           for x in g]
    m_rb = [jnp.where(incl, x[C2:, :C2], 0.0).astype(BF16) for x in g]

    l_bf = [x.astype(BF16) for x in l_ab]
    pw = [_dot(x, x) for x in l_bf]
    acc = [eye + x for x in l_ab]
    for _ in range(1, int(math.log2(C)) - 1):
        ps = [_dot(p.astype(BF16), cat([s, p], 1).astype(BF16)) for p, s in zip(pw, acc)]
        acc = [s + x[:, :C2] for s, x in zip(acc, ps)]
        pw = [x[:, C2:] for x in ps]
    acc = [s + _dot(p.astype(BF16), s.astype(BF16)) for p, s in zip(pw, acc)]

    lm = [_dot(x, vv) for x, vv in zip(lmk, v_bf)]
    av = [_dot(t.astype(BF16), cat([a, x[:C2]], 1).astype(BF16)).astype(BF16)
          for t, a, x in zip(acc, a_st, lm)]
    ry = [_dot(m, x) + cat([r, y[C2:]], 1) for m, x, r, y in zip(m_rb, av, r_st, lm)]
    zero = jnp.zeros((C2, C2), BF16)
    md = [_dot_tn(yh, cat([x, cat([zero, vv], 1)], 0)) for yh, x, vv in zip(y_h, av, v_bf)]
    rmp = [cat([y[:, :C2], jnp.where(diag, jnp.broadcast_to(pe, (C2, C2)), 0.0) + x[:, :C2]], 0).astype(BF16)
           for y, pe, x in zip(ry, p_end, md)]

    h = [h_ref[pr] for pr in pairs]
    for ci_, rows in enumerate(row_list):
        base = ci_ * len(pairs)
        rm = [_dot(rmp[base + pr], h[pr].astype(BF16)) for pr in pairs]
        h = [rm[pr][C2:] + md[base + pr][:, C2:] for pr in pairs]
        for pr in pairs:
            y_st = rm[pr][:C2] + ry[base + pr][:, C2:]
            y_s[rows, lanes[pr]] = y_st[:C] + y_st[C:]
    for pr in pairs:
        h_ref[pr] = h[pr]


def _rwkv_kernel(z_ref, zh_ref, mu_ref, w0_ref, a0_ref, w2a_ref, g2_ref, kk_ref, ka_ref, rk_ref,
                 gnw_ref, gnb_ref, seg_ref, tril_ref, y_ref,
                 h_ref, r_s, k_s, v_s, na_s, b_s, lw_s, cum_s, g_s, y_s):
    t = pl.program_id(1)
    TS = RWKV_TS
    D = RWKV_DIM

    @pl.when(t == 0)
    def _():
        h_ref[...] = jnp.zeros_like(h_ref)

    has_prev = jnp.where(t > 0, 1.0, 0.0).astype(F32)

    def shifted(lo, hi):
        zc = z_ref[:, lo:hi]
        prev = zh_ref[SUBLANES - 1:SUBLANES, lo:hi] * has_prev
        rolled = pltpu.roll(zc, 1, 0)
        row = lax.broadcasted_iota(jnp.int32, (SUBLANES, hi - lo), 0)
        zp = jnp.concatenate([jnp.where(row == 0, prev, rolled[:SUBLANES]), rolled[SUBLANES:]], axis=0)
        return zc + (zp - zc) * mu_ref[:, lo:hi]

    seg = seg_ref[...]
    r = shifted(0, D)
    k = shifted(D, 2 * D)
    v = shifted(2 * D, 3 * D)
    wa = shifted(3 * D, 3 * D + DECAY_LORA + ICLR_LORA)
    gd = shifted(3 * D + DECAY_LORA + ICLR_LORA, RWKV_COLS)

    lane = lax.broadcasted_iota(jnp.int32, wa.shape, 1)
    lora_in = jnp.where(lane < DECAY_LORA, jnp.tanh(wa), wa).astype(BF16)
    lora = _dot(lora_in, w2a_ref[...])
    lw = -math.exp(-0.5) * _sigmoid(w0_ref[...] + lora[:, :D])
    lw_s[...] = lw
    tril = tril_ref[...]
    hi = lw.astype(BF16)
    lo = (lw - hi.astype(F32)).astype(BF16)
    span = tril.shape[0]
    for r0 in range(0, TS, span):
        cum_s[r0:r0 + span, :] = _dot(tril, hi[r0:r0 + span]) + _dot(tril, lo[r0:r0 + span])
    a = _sigmoid(a0_ref[...] + lora[:, D:])
    g_s[...] = _dot(_sigmoid(gd).astype(BF16), g2_ref[...])

    kk = k * kk_ref[...]
    kk = kk * lax.rsqrt(jnp.maximum(_segsum(kk * kk, seg), 1e-24))
    k2 = k * (1.0 + (a - 1.0) * ka_ref[...])
    r_s[...] = r
    k_s[...] = k2
    v_s[...] = v
    na_s[...] = -kk
    b_s[...] = kk * a

    row_list = [slice(c * CHUNK, (c + 1) * CHUNK) for c in range(TS // CHUNK)]
    _scan_chunks(row_list, h_ref, r_s, k_s, v_s, na_s, b_s, lw_s, cum_s, y_s)

    y = y_s[...]
    inv_n = 1.0 / HEAD_DIM
    mean = _segsum(y, seg) * inv_n
    yc = y - mean
    var = _segsum(yc * yc, seg) * inv_n
    yn = yc * lax.rsqrt(var + GN_EPS) * gnw_ref[...] + gnb_ref[...]
    bonus = _segsum(r_s[...] * k_s[...] * rk_ref[...], seg) * v_s[...]
    y_ref[...] = ((yn + bonus) * g_s[...]).astype(BF16)


def _rwkv(zr, mu, w0, a0, w2a, g2, k_k, k_a, r_k, gn_w, gn_b, seg, B, S):
    TS = RWKV_TS
    t_idx = np.arange(2 * CHUNK)
    tril = jnp.asarray((t_idx[:, None] // CHUNK == t_idx[None, :] // CHUNK)
                       & (t_idx[None, :] <= t_idx[:, None]), BF16)
    consts = [mu, w0, a0, w2a, g2, k_k, k_a, r_k, gn_w, gn_b, seg, tril]
    halo_blocks = TS // SUBLANES
    stage = pltpu.VMEM((TS, RWKV_DIM), F32)
    return pl.pallas_call(
        _rwkv_kernel,
        grid=(B, S // TS),
        in_specs=[pl.BlockSpec((None, TS, RWKV_COLS), lambda b, t: (b, t, 0)),
                  pl.BlockSpec((None, SUBLANES, RWKV_COLS),
                               lambda b, t: (b, jnp.maximum(t * halo_blocks - 1, 0), 0))]
                 + [_const_spec(c) for c in consts],
        out_specs=pl.BlockSpec((None, TS, RWKV_DIM), lambda b, t: (b, t, 0)),
        out_shape=jax.ShapeDtypeStruct((B, S, RWKV_DIM), BF16),
        scratch_shapes=[pltpu.VMEM((RWKV_HEADS // 2, 2 * CHUNK, 2 * CHUNK), F32)] + [stage] * 9,
        compiler_params=_params("parallel", "arbitrary"),
        name="rwkv",
    )(zr, zr, *consts)


Q_SLAB, K_SLAB, V_SLAB, N_SLABS = 0, 4, 6, 10


def _fold_rows(tile, sub, dil):
    return pl.ds(tile * (ATTN_ROWS // dil) + sub, dil, stride=FOLD_PITCH)


def _attn_group(gi, dil, S, x_s, o_s, l_s, eye_ref, bias_ref):
    L = S // dil
    nb = L // BAND_BLOCK
    Q = BAND_BLOCK
    heads = range(HEADS_PER_GROUP)
    lane2 = lax.broadcasted_iota(jnp.int32, (Q, 2 * LANES), 1)
    hmask = [lane2 // HEAD_DIM == j for j in heads]
    eye = eye_ref[...]

    def load(slab, rows):
        return x_s[slab, rows, :].astype(BF16)

    units = [divmod(u, nb) for u in range(dil * nb)]
    cur, q, k_aug, v = [], [], [], []
    for res, blk in units:
        if nb > 1:
            rows = pl.ds(res + blk * (Q * dil), Q, stride=dil)
        else:
            rows = pl.ds(res * FOLD_PITCH, Q)
        if blk > 0:
            prev = pl.ds(res + (blk - 1) * (Q * dil), Q, stride=dil)
            both = lambda slab, prev=prev, rows=rows: jnp.concatenate([load(slab, prev), load(slab, rows)], axis=0)
            bias = bias_ref[...]
        else:
            both = lambda slab, rows=rows: load(slab, rows)
            bias = bias_ref[Q:, :]
        cur.append(rows)
        q.append([jnp.concatenate([load(Q_SLAB + j, rows), eye], axis=1) for j in heads])
        k_aug.append([jnp.concatenate([both(K_SLAB + hp), bias], axis=1) for hp in range(2)])
        v.append([both(V_SLAB + j) for j in heads])
    chains = [(n, j) for n in range(len(units)) for j in heads]
    s = [_dot_nt(q[n][j], k_aug[n][j // 2]) for n, j in chains]
    m = [jnp.max(x, axis=-1, keepdims=True) for x in s]
    e = [jnp.exp2(x - y) for x, y in zip(s, m)]
    l = [jnp.sum(x, axis=-1, keepdims=True) for x in e]
    p = [(x * (1.0 / y)).astype(BF16) for x, y in zip(e, l)]
    for n in range(len(units)):
        c = n * HEADS_PER_GROUP
        lse = jnp.broadcast_to(m[c] * LN2 + jnp.log(l[c]), (Q, 2 * LANES))
        for j in heads[1:]:
            lse = jnp.where(hmask[j], m[c + j] * LN2 + jnp.log(l[c + j]), lse)
        for hp in range(2):
            pp = jnp.concatenate([p[c + 2 * hp], p[c + 2 * hp + 1]], axis=1)
            vv = jnp.concatenate([v[n][2 * hp], v[n][2 * hp + 1]], axis=0)
            o_s[gi, hp, cur[n], :] = _dot(pp, vv)
            l_s[gi, hp, cur[n], :] = lse[:, hp * LANES:(hp + 1) * LANES]


def _attn_kernel(q_ref, k_ref, v_ref, pos_ref, invf_ref, qg_ref, qgp_ref, kg_ref, kgp_ref, seg_ref, rot_ref,
                 eye_ref, bias_ref, spread_ref, out_ref, x_s, o_s, l_s, cs_s):
    g = pl.program_id(1)
    S = q_ref.shape[0]
    R = ATTN_ROWS
    ng = len(ATTN_GROUPS)
    fold_dil = ATTN_GROUPS[-1][1]

    @pl.when(g == 0)
    def _():
        rc = R

        def body(i, carry):
            ang = pos_ref[pl.ds(pl.multiple_of(i * rc, rc), rc), :].astype(F32) * invf_ref[...]
            for tbl, fn in enumerate((jnp.cos, jnp.sin)):
                val = fn(ang)
                hi = val.astype(BF16)
                rest = val - hi.astype(F32)
                mid = rest.astype(BF16)
                lo = (rest - mid.astype(F32)).astype(BF16)
                q_gain, k_gain = ((qg_ref, kg_ref), (qgp_ref, kgp_ref))[tbl]
                for sub in range(POS_PER_ROW):
                    spread = spread_ref[sub]
                    full = _dot(hi, spread) + _dot(mid, spread) + _dot(lo, spread)
                    rows = pl.ds(i * (rc * POS_PER_ROW) + sub, rc, stride=POS_PER_ROW)
                    cs_s[tbl, rows, :] = full * (q_gain[...] * (HEAD_DIM ** -0.5 * LOG2E))
                    cs_s[2 + tbl, rows, :] = full * k_gain[...]
            return carry
        lax.fori_loop(0, S // (rc * POS_PER_ROW), body, 0)

    seg = seg_ref[...]
    rot = rot_ref[...]
    first = lax.broadcasted_iota(jnp.int32, (R, LANES), 1) < HEAD_DIM

    def prep(folded, i, carry):
        rows = pl.ds(pl.multiple_of(i * R, R), R)

        def store(slab, val):
            if folded:
                for sub in range(R // fold_dil):
                    x_s[slab, _fold_rows(i, sub, fold_dil), :] = val[sub * fold_dil:(sub + 1) * fold_dil]
            else:
                x_s[slab, rows, :] = val

        def store_per_head(slab, val):
            store(slab, jnp.where(first, val, 0.0))
            store(slab + 1, jnp.where(first, 0.0, val))

        def normed(src, hf, tables):
            x = src[rows, hf * LANES:(hf + 1) * LANES]
            xf = x.astype(F32)
            inv = lax.rsqrt(_segsum(xf * xf, seg) + RMS_EPS)
            xr = _dot(x, rot)
            return (xf * cs_s[tables, rows, :] + xr * cs_s[tables + 1, rows, :]) * inv

        for hf in range(2):
            store_per_head(Q_SLAB + 2 * hf, normed(q_ref, hf, 0))
            store(K_SLAB + hf, normed(k_ref, hf, 2))
            store_per_head(V_SLAB + 2 * hf, v_ref[rows, hf * LANES:(hf + 1) * LANES].astype(F32))
        return carry

    @pl.when(g < ng - 1)
    def _():
        lax.fori_loop(0, S // R, functools.partial(prep, False), 0)

    @pl.when(g == ng - 1)
    def _():
        lax.fori_loop(0, S // R, functools.partial(prep, True), 0)

    for gi, (_, dil) in enumerate(ATTN_GROUPS):
        pl.when(g == gi)(functools.partial(_attn_group, gi, dil, S, x_s, o_s, l_s, eye_ref, bias_ref))

    @pl.when(g == ng - 1)
    def _():
        def body(i, carry):
            rows = pl.ds(pl.multiple_of(i * R, R), R)

            def fetch(ref, gi, slab):
                if gi < ng - 1:
                    return ref[gi, slab, rows, :]
                return jnp.concatenate([ref[gi, slab, _fold_rows(i, sub, fold_dil), :]
                                        for sub in range(R // fold_dil)], axis=0)

            for slab in range(2):
                ls = [fetch(l_s, gi, slab) for gi in range(ng)]
                m = jnp.maximum(jnp.maximum(ls[0], ls[1]), ls[2])
                ws = [jnp.exp(x - m) for x in ls]
                num = sum(w * fetch(o_s, gi, slab) for gi, w in enumerate(ws))
                out_ref[rows, slab * LANES:(slab + 1) * LANES] = (num / (ws[0] + ws[1] + ws[2])).astype(BF16)
            return carry
        lax.fori_loop(0, S // R, body, 0)


def _attn(za, pos, consts, B, S):
    ng = len(ATTN_GROUPS)
    gw = ATTN_OUT_DIM
    rows = (S // BAND_BLOCK) * FOLD_PITCH
    return pl.pallas_call(
        _attn_kernel,
        grid=(B, ng),
        in_specs=[pl.BlockSpec((None, S, gw), lambda b, g: (b, 0, g)),
                  pl.BlockSpec((None, S, gw), lambda b, g: (b, 0, ng + g)),
                  pl.BlockSpec((None, S, gw), lambda b, g: (b, 0, 2 * ng + g)),
                  pl.BlockSpec((None, S // POS_PER_ROW, LANES), lambda b, g: (b, 0, 0))]
                 + [_const_spec(c) for c in consts],
        out_specs=pl.BlockSpec((None, S, ATTN_OUT_DIM), lambda b, g: (b, 0, 0)),
        out_shape=jax.ShapeDtypeStruct((B, S, ATTN_OUT_DIM), BF16),
        scratch_shapes=[pltpu.VMEM((N_SLABS, rows, LANES), F32), pltpu.VMEM((ng, 2, rows, LANES), F32),
                        pltpu.VMEM((ng, 2, rows, LANES), F32), pltpu.VMEM((4, S, LANES), F32)],
        compiler_params=_params("parallel", "arbitrary"),
        name="attn",
    )(za, za, za, pos, *consts)


def _out_kernel(x1_ref, gt_ref, yr_ref, ya_ref, p_ref, wbr_ref, wba_ref, wo_ref, n2_ref, wg_ref, wu_ref,
                wd_ref, pn_ref, pwg_ref, pwp_ref, out_ref):
    br = _dot(yr_ref[...], wbr_ref[...])
    ba = _dot(ya_ref[...], wba_ref[...])
    merged = gt_ref[:, :D_MODEL].astype(F32) * br + gt_ref[:, D_MODEL:].astype(F32) * ba
    x2 = x1_ref[...] + _dot(merged.astype(BF16), wo_ref[...])
    h = _rms(x2, n2_ref[...]).astype(BF16)
    g = _dot(h, wg_ref[...])
    u = _dot(h, wu_ref[...])
    act = (g * _sigmoid(g) * u).astype(BF16)
    x3 = x2 + 0.5 * _dot(act, wd_ref[...])
    hp = _rms(x3, pn_ref[...]).astype(BF16)
    gate = _sigmoid(_dot(hp, pwg_ref[...]))
    out_ref[...] = x3 + gate * _dot(p_ref[...].astype(BF16), pwp_ref[...])


def _out(x1, gt, yr, ya, p2d, consts):
    T = x1.shape[0]
    TM = OUT_TM
    row = lambda i: (i, 0)
    return pl.pallas_call(
        _out_kernel,
        grid=(T // TM,),
        in_specs=[pl.BlockSpec((TM, D_MODEL), row), pl.BlockSpec((TM, GATE_COLS), row),
                  pl.BlockSpec((TM, RWKV_DIM), row), pl.BlockSpec((TM, ATTN_OUT_DIM), row),
                  pl.BlockSpec((TM, PLE_DIM), row)] + [_const_spec(c) for c in consts],
        out_specs=pl.BlockSpec((TM, D_MODEL), row),
        out_shape=jax.ShapeDtypeStruct((T, D_MODEL), F32),
        compiler_params=_params("parallel"),
        name="merge_ffn2",
    )(x1, gt, yr, ya, p2d, *consts)


def _rotate_half_matrix():
    half = HEAD_DIM // 2
    m = np.zeros((LANES, LANES), np.float32)
    for lane in range(LANES):
        if lane % HEAD_DIM < half:
            m[lane + half, lane] = -1.0
        else:
            m[lane - half, lane] = 1.0
    return jnp.asarray(m, BF16)


def _band_bias():
    kq = np.arange(BAND_BLOCK)[:, None]
    qq = np.arange(BAND_BLOCK)[None, :]
    cur = np.where(kq <= qq, 0.0, NEG_INF)
    prev = np.where(kq >= qq, 0.0, NEG_INF)
    return jnp.asarray(np.concatenate([prev, cur]), BF16)


def _block_ones(n, blk):
    idx = np.arange(n) // blk
    return jnp.asarray(idx[:, None] == idx[None, :], BF16)


def kernel(x, p, positions, ffn1_norm, ffn1_w_gate, ffn1_w_up, ffn1_w_down, mix_norm, w_in, rwkv_mu, rwkv_w0, rwkv_w2, rwkv_a0, rwkv_a2, rwkv_g2, rwkv_k_k, rwkv_k_a, rwkv_r_k, rwkv_gn_w, rwkv_gn_b, q_norm, k_norm, w_br_rwkv, w_br_attn, w_out, ffn2_norm, ffn2_w_gate, ffn2_w_up, ffn2_w_down, ple_norm, ple_w_gate, ple_w_proj):
    B, S, D = x.shape
    depth = p.shape[0]
    T = B * S
    half = HEAD_DIM // 2
    bf = lambda w: w.astype(BF16)
    rowvec = lambda a: a.reshape(1, -1).astype(F32)

    inv_freq = 1.0 / (ROPE_THETA ** (jnp.arange(0, HEAD_DIM, 2, dtype=F32) / HEAD_DIM))
    invf = jnp.tile(inv_freq, HEADS_PER_GROUP).reshape(1, LANES)
    pos = jnp.repeat(positions.reshape(B, S // POS_PER_ROW, POS_PER_ROW), half, axis=2)
    lane = np.arange(LANES)
    spread = jnp.asarray(np.stack([lane[:, None] == sub * half + lane[None, :] % half
                                   for sub in range(POS_PER_ROW)]), BF16)
    seg64 = _block_ones(2 * LANES, HEAD_DIM)
    seg_mean = _block_ones(LANES, HEAD_DIM) * (1.0 / HEAD_DIM)
    rot = _rotate_half_matrix()
    eye = jnp.eye(BAND_BLOCK, dtype=BF16)
    heads_per_slab = LANES // HEAD_DIM
    gain = lambda a: jnp.tile(a, heads_per_slab).reshape(1, LANES).astype(F32)
    partner = lambda a: gain(jnp.concatenate([a[half:], a[:half]]))

    xc = x.reshape(T, D)
    for i in range(depth):
        zeros = jnp.zeros((DECAY_LORA, RWKV_DIM), F32)
        w2a = jnp.concatenate([jnp.concatenate([rwkv_w2[i], zeros], axis=1),
                               jnp.concatenate([zeros, rwkv_a2[i]], axis=1)], axis=0)

        x1, h2, w_in_bf = _ffn1(xc, rowvec(ffn1_norm[i]), bf(ffn1_w_gate[i]), bf(ffn1_w_up[i]), bf(ffn1_w_down[i]),
                                rowvec(mix_norm[i]), [w_in[i]])
        out_weights = [w_br_rwkv[i], w_br_attn[i], w_out[i], ffn2_w_gate[i], ffn2_w_up[i], ffn2_w_down[i],
                       ple_w_gate[i], ple_w_proj[i]]
        zr, za, gt, wbr, wba, wo, wg2, wu2, wd2, pwg, pwp = _inproj(h2, w_in_bf, out_weights)
        yr = _rwkv(zr.reshape(B, S, RWKV_COLS), rowvec(rwkv_mu[i]), rowvec(rwkv_w0[i]), rowvec(rwkv_a0[i]),
                   bf(w2a), bf(rwkv_g2[i]), rowvec(rwkv_k_k[i]), rowvec(rwkv_k_a[i]), rowvec(rwkv_r_k[i]),
                   rowvec(rwkv_gn_w[i]), rowvec(rwkv_gn_b[i]), seg64, B, S)
        attn_consts = [invf, gain(q_norm[i]), partner(q_norm[i]), gain(k_norm[i]), partner(k_norm[i]),
                       seg_mean, rot, eye, _band_bias(), spread]
        ya = _attn(za.reshape(B, S, ATTN_COLS), pos, attn_consts, B, S)
        consts = [wbr, wba, wo, rowvec(ffn2_norm[i]), wg2, wu2, wd2, rowvec(ple_norm[i]), pwg, pwp]
        xc = _out(x1, gt, yr.reshape(T, RWKV_DIM), ya.reshape(T, ATTN_OUT_DIM), p[i].reshape(T, PLE_DIM), consts)
    return xc.reshape(B, S, D)
```

```python
import functools
import math

import numpy as np
import jax
import jax.numpy as jnp
from jax import lax
from jax.experimental import pallas as pl
from jax.experimental.pallas import tpu as pltpu

F32 = jnp.float32
BF16 = jnp.bfloat16

D_MODEL = 1024
PLE_DIM = 256
HEAD_DIM = 64
RWKV_HEADS = 8
RWKV_DIM = RWKV_HEADS * HEAD_DIM
DECAY_LORA = 64
ICLR_LORA = 64
GATE_LORA = 128
GN_EPS = 64e-5
ATTN_GROUPS = ((128, 1), (512, 4), (2048, 16))
HEADS_PER_GROUP = 4
ATTN_DIM = HEADS_PER_GROUP * len(ATTN_GROUPS) * HEAD_DIM
ATTN_OUT_DIM = HEADS_PER_GROUP * HEAD_DIM
BAND_BLOCK = 128
ROPE_THETA = 10000.0
NEG_INF = -1e30
D_FF = 2816
RMS_EPS = 1e-6
LOG2E = math.log2(math.e)
LN2 = math.log(2.0)
RWKV_COLS = 3 * RWKV_DIM + DECAY_LORA + ICLR_LORA + GATE_LORA
ATTN_COLS = 3 * ATTN_DIM
GATE_COLS = 2 * D_MODEL

LANES = 128
SUBLANES = 8
BF16_ROWS = 16
VMEM_LIMIT_BYTES = 60 * 1024 * 1024

FFN_TM = 512
PROJ_TM = 1024
OUT_TM = 512
RWKV_TS = 512
CHUNK = 64
ROW_PARTS = 4
ATTN_ROWS = 256
FOLD_PITCH = BAND_BLOCK + SUBLANES
POS_PER_ROW = LANES // (HEAD_DIM // 2)


def _dot(a, b):
    return jnp.dot(a, b, preferred_element_type=F32)


def _dot_nt(a, b):
    return lax.dot_general(a, b, (((1,), (1,)), ((), ())), preferred_element_type=F32)


def _dot_tn(a, b):
    return lax.dot_general(a, b, (((0,), (0,)), ((), ())), preferred_element_type=F32)


def _rms(x, gain):
    return x * lax.rsqrt(jnp.mean(x * x, axis=-1, keepdims=True) + RMS_EPS) * gain


def _sigmoid(x):
    return 1.0 / (1.0 + jnp.exp(-x))


def _segsum(x, seg):
    w = seg.shape[0]
    parts = [_dot(x[:, lo:lo + w].astype(BF16), seg) for lo in range(0, x.shape[1], w)]
    return parts[0] if len(parts) == 1 else jnp.concatenate(parts, axis=1)


def _const_spec(arr):
    nd = arr.ndim
    return pl.BlockSpec(arr.shape, lambda *_: (0,) * nd, pipeline_mode=pl.Buffered(1))


def _params(*sem):
    return pltpu.CompilerParams(dimension_semantics=sem, vmem_limit_bytes=VMEM_LIMIT_BYTES)


def _row_parts(n):
    return [slice(i * n // ROW_PARTS, (i + 1) * n // ROW_PARTS) for i in range(ROW_PARTS)]


def _cast_plan(weights, nsteps):
    specs, shapes = [], []
    for w in weights:
        rows, cols = w.shape
        every = 1
        while (rows * every) % nsteps or (rows * every // nsteps) % BF16_ROWS:
            every *= 2
        specs.append(pl.BlockSpec((rows * every // nsteps, cols), lambda i, every=every: (i // every, 0)))
        shapes.append(jax.ShapeDtypeStruct(w.shape, BF16))
    return specs, shapes


def _cast_slabs(src_refs, dst_refs):
    for src, dst in zip(src_refs, dst_refs):
        dst[...] = src[...].astype(BF16)


def _ffn1_kernel(ncast, x_ref, n1_ref, wg_ref, wu_ref, wd_ref, n2_ref, *rest):
    cast_in, (x1_ref, h2_ref), cast_out = rest[:ncast], rest[ncast:ncast + 2], rest[ncast + 2:]
    _cast_slabs(cast_in, cast_out)
    for rows in _row_parts(x_ref.shape[0]):
        x = x_ref[rows, :]
        h = _rms(x, n1_ref[...]).astype(BF16)
        g = _dot(h, wg_ref[...])
        u = _dot(h, wu_ref[...])
        act = (g * _sigmoid(g) * u).astype(BF16)
        x1 = x + 0.5 * _dot(act, wd_ref[...])
        x1_ref[rows, :] = x1
        h2_ref[rows, :] = _rms(x1, n2_ref[...]).astype(BF16)


def _ffn1(x2d, n1, wg, wu, wd, n2, next_weights):
    T = x2d.shape[0]
    nsteps = T // FFN_TM
    row = lambda i: (i, 0)
    cast_specs, cast_shapes = _cast_plan(next_weights, nsteps)
    return pl.pallas_call(
        functools.partial(_ffn1_kernel, len(next_weights)),
        grid=(nsteps,),
        in_specs=[pl.BlockSpec((FFN_TM, D_MODEL), row), _const_spec(n1), _const_spec(wg),
                  _const_spec(wu), _const_spec(wd), _const_spec(n2)] + cast_specs,
        out_specs=[pl.BlockSpec((FFN_TM, D_MODEL), row), pl.BlockSpec((FFN_TM, D_MODEL), row)] + cast_specs,
        out_shape=[jax.ShapeDtypeStruct((T, D_MODEL), F32), jax.ShapeDtypeStruct((T, D_MODEL), BF16)]
                  + cast_shapes,
        compiler_params=_params("arbitrary"),
        name="ffn1",
    )(x2d, n1, wg, wu, wd, n2, *next_weights)


def _inproj_kernel(ncast, h_ref, w_ref, *rest):
    cast_in, (zr_ref, za_ref, gt_ref), cast_out = rest[:ncast], rest[ncast:ncast + 3], rest[ncast + 3:]
    _cast_slabs(cast_in, cast_out)
    for rows in _row_parts(h_ref.shape[0]):
        h = h_ref[rows, :]
        zr_ref[rows, :] = _dot(h, w_ref[:, :RWKV_COLS])
        za_ref[rows, :] = _dot(h, w_ref[:, RWKV_COLS:RWKV_COLS + ATTN_COLS]).astype(BF16)
        gt_ref[rows, :] = _sigmoid(_dot(h, w_ref[:, RWKV_COLS + ATTN_COLS:])).astype(BF16)


def _inproj(h2, w, next_weights):
    T = h2.shape[0]
    nsteps = T // PROJ_TM
    row = lambda i: (i, 0)
    cast_specs, cast_shapes = _cast_plan(next_weights, nsteps)
    return pl.pallas_call(
        functools.partial(_inproj_kernel, len(next_weights)),
        grid=(nsteps,),
        in_specs=[pl.BlockSpec((PROJ_TM, D_MODEL), row), _const_spec(w)] + cast_specs,
        out_specs=[pl.BlockSpec((PROJ_TM, RWKV_COLS), row), pl.BlockSpec((PROJ_TM, ATTN_COLS), row),
                   pl.BlockSpec((PROJ_TM, GATE_COLS), row)] + cast_specs,
        out_shape=[jax.ShapeDtypeStruct((T, RWKV_COLS), F32), jax.ShapeDtypeStruct((T, ATTN_COLS), BF16),
                   jax.ShapeDtypeStruct((T, GATE_COLS), BF16)] + cast_shapes,
        compiler_params=_params("arbitrary"),
        name="inproj",
    )(h2, w, *next_weights)


def _scan_chunks(row_list, h_ref, r_s, k_s, v_s, na_s, b_s, lw_s, cum_s, y_s):
    C = CHUNK
    C2 = 2 * C
    pairs = range(RWKV_HEADS // 2)
    rows_of = [rows for rows in row_list for _ in pairs]
    lanes = [slice(pr * LANES, (pr + 1) * LANES) for _ in row_list for pr in pairs]
    ci = lax.broadcasted_iota(jnp.int32, (C, LANES), 1)
    ri = lax.broadcasted_iota(jnp.int32, (C, LANES), 0)
    first = ci < HEAD_DIM
    strict = ci % C < ri
    incl = ci % C <= ri
    eye = jnp.where(ci % C == ri, 1.0, 0.0)
    ri2 = lax.broadcasted_iota(jnp.int32, (C2, 2 * C2), 0)
    ci2 = lax.broadcasted_iota(jnp.int32, (C2, 2 * C2), 1)
    same_head = ri2 // C == (ci2 % C2) // C
    diag = lax.broadcasted_iota(jnp.int32, (C2, C2), 0) == lax.broadcasted_iota(jnp.int32, (C2, C2), 1)

    def stack(x):
        return jnp.concatenate([jnp.where(first, x, 0.0), jnp.where(first, 0.0, x)], axis=0)

    def cat(parts, axis):
        return jnp.concatenate(parts, axis=axis)

    def bf(x):
        return x.astype(BF16)

    def swap(x):
        return pltpu.roll(x, HEAD_DIM, 1)

    def load(ref):
        return [ref[rows, ln] for rows, ln in zip(rows_of, lanes)]

    lw = load(lw_s)
    cum = load(cum_s)
    p_inc = [jnp.exp(x) for x in cum]
    p_exc = [jnp.exp(x - y) for x, y in zip(cum, lw)]
    p_inv = [jnp.exp(-x) for x in cum]
    p_end = [x[C - 1:C, :] for x in p_inc]
    to_end = [x * y for x, y in zip(p_inv, p_end)]

    a_st = [stack(x * p) for x, p in zip(load(na_s), p_exc)]
    r_pk = [x * p for x, p in zip(load(r_s), p_inc)]
    v_pk = load(v_s)
    v_st = [bf(stack(x)) for x in v_pk]
    b = load(b_s)
    k = load(k_s)
    xr = [bf(cat([a, stack(r)], 0)) for a, r in zip(a_st, r_pk)]
    y_g = [bf(cat([bb * p, kk * p], 0)) for bb, kk, p in zip(b, k, p_inv)]
    y_h = [bf(cat([bb * p, kk * p], 0)) for bb, kk, p in zip(b, k, to_end)]

    g = [_dot_nt(x, y) for x, y in zip(xr, y_g)]
    l_ab = [jnp.where(strict & first, x[:C], jnp.where(strict, swap(x[C:C2]), 0.0)) for x in g]
    l_ak = [jnp.where(strict & first, swap(x[:C]), jnp.where(strict, x[C:C2], 0.0)) for x in g]
    m_rb = [jnp.where(incl & first, x[C2:C2 + C], jnp.where(incl, swap(x[C2 + C:]), 0.0)) for x in g]
    m_rk = [jnp.where(incl & first, swap(x[C2:C2 + C]), jnp.where(incl, x[C2 + C:], 0.0)) for x in g]

    pw = [_dot(bf(x), bf(stack(x))) for x in l_ab]
    acc = [eye + x for x in l_ab]
    for _ in range(1, int(math.log2(C)) - 1):
        tp = [_dot(bf(cat([t, p], 0)), bf(stack(p))) for t, p in zip(acc, pw)]
        acc = [t + x[:C] for t, x in zip(acc, tp)]
        pw = [x[C:] for x in tp]
    acc = [t + _dot(bf(t), bf(stack(p))) for t, p in zip(acc, pw)]

    lm = [_dot(bf(cat([x, y], 0)), vv) for x, y, vv in zip(l_ak, m_rk, v_st)]
    av = [_dot(bf(t), bf(cat([a, stack(x[:C])], 1))) for t, a, x in zip(acc, a_st, lm)]
    ry = [_dot(bf(m), bf(cat([stack(x[:, :LANES]), stack(x[:, LANES:])], 1))) + cat([r, y[C:]], 1)
          for m, x, r, y in zip(m_rb, av, r_pk, lm)]
    zero = jnp.zeros((C, LANES), F32)
    md = [jnp.where(same_head, _dot_tn(yh, bf(cat([x, cat([zero, vv], 1)], 0))), 0.0)
          for yh, x, vv in zip(y_h, av, v_pk)]
    r_new = [bf(y[:, :LANES]) for y in ry]
    m_new = [bf(jnp.where(diag, jnp.broadcast_to(pe, (C2, C2)), 0.0) + x[:, :C2])
             for pe, x in zip(p_end, md)]

    h = [h_ref[pr] for pr in pairs]
    for ci_, rows in enumerate(row_list):
        base = ci_ * len(pairs)
        hb = [bf(x) for x in h]
        yy = [_dot(r_new[base + pr], hb[pr]) for pr in pairs]
        h = [_dot(m_new[base + pr], hb[pr]) + md[base + pr][:, C2:] for pr in pairs]
        for pr in pairs:
            y_s[rows, lanes[pr]] = yy[pr] + ry[base + pr][:, LANES:]
    for pr in pairs:
        h_ref[pr] = h[pr]


def _rwkv_kernel(z_ref, zh_ref, mu_ref, w0_ref, a0_ref, w2a_ref, g2_ref, kk_ref, ka_ref, rk_ref,
                 gnw_ref, gnb_ref, seg_ref, tril_ref, y_ref,
                 h_ref, r_s, k_s, v_s, na_s, b_s, lw_s, cum_s, g_s, y_s):
    t = pl.program_id(1)
    TS = RWKV_TS
    D = RWKV_DIM

    @pl.when(t == 0)
    def _():
        h_ref[...] = jnp.zeros_like(h_ref)

    has_prev = jnp.where(t > 0, 1.0, 0.0).astype(F32)

    def shifted(lo, hi):
        zc = z_ref[:, lo:hi]
        prev = zh_ref[SUBLANES - 1:SUBLANES, lo:hi] * has_prev
        rolled = pltpu.roll(zc, 1, 0)
        row = lax.broadcasted_iota(jnp.int32, (SUBLANES, hi - lo), 0)
        zp = jnp.concatenate([jnp.where(row == 0, prev, rolled[:SUBLANES]), rolled[SUBLANES:]], axis=0)
        return zc + (zp - zc) * mu_ref[:, lo:hi]

    seg = seg_ref[...]
    r = shifted(0, D)
    k = shifted(D, 2 * D)
    v = shifted(2 * D, 3 * D)
    wa = shifted(3 * D, 3 * D + DECAY_LORA + ICLR_LORA)
    gd = shifted(3 * D + DECAY_LORA + ICLR_LORA, RWKV_COLS)

    lane = lax.broadcasted_iota(jnp.int32, wa.shape, 1)
    lora_in = jnp.where(lane < DECAY_LORA, jnp.tanh(wa), wa).astype(BF16)
    lora = _dot(lora_in, w2a_ref[...])
    lw = -math.exp(-0.5) * _sigmoid(w0_ref[...] + lora[:, :D])
    lw_s[...] = lw
    tril = tril_ref[...]
    hi = lw.astype(BF16)
    lo = (lw - hi.astype(F32)).astype(BF16)
    span = tril.shape[0]
    for r0 in range(0, TS, span):
        cum_s[r0:r0 + span, :] = _dot(tril, hi[r0:r0 + span]) + _dot(tril, lo[r0:r0 + span])
    a = _sigmoid(a0_ref[...] + lora[:, D:])
    g_s[...] = _dot(_sigmoid(gd).astype(BF16), g2_ref[...])

    kk = k * kk_ref[...]
    kk = kk * lax.rsqrt(jnp.maximum(_segsum(kk * kk, seg), 1e-24))
    k2 = k * (1.0 + (a - 1.0) * ka_ref[...])
    r_s[...] = r
    k_s[...] = k2
    v_s[...] = v
    na_s[...] = -kk
    b_s[...] = kk * a

    row_list = [slice(c * CHUNK, (c + 1) * CHUNK) for c in range(TS // CHUNK)]
    _scan_chunks(row_list, h_ref, r_s, k_s, v_s, na_s, b_s, lw_s, cum_s, y_s)

    y = y_s[...]
    inv_n = 1.0 / HEAD_DIM
    mean = _segsum(y, seg) * inv_n
    yc = y - mean
    var = _segsum(yc * yc, seg) * inv_n
    yn = yc * lax.rsqrt(var + GN_EPS) * gnw_ref[...] + gnb_ref[...]
    bonus = _segsum(r_s[...] * k_s[...] * rk_ref[...], seg) * v_s[...]
    y_ref[...] = ((yn + bonus) * g_s[...]).astype(BF16)


def _rwkv(zr, mu, w0, a0, w2a, g2, k_k, k_a, r_k, gn_w, gn_b, seg, B, S):
    TS = RWKV_TS
    t_idx = np.arange(2 * CHUNK)
    tril = jnp.asarray((t_idx[:, None] // CHUNK == t_idx[None, :] // CHUNK)
                       & (t_idx[None, :] <= t_idx[:, None]), BF16)
    consts = [mu, w0, a0, w2a, g2, k_k, k_a, r_k, gn_w, gn_b, seg, tril]
    halo_blocks = TS // SUBLANES
    stage = pltpu.VMEM((TS, RWKV_DIM), F32)
    return pl.pallas_call(
        _rwkv_kernel,
        grid=(B, S // TS),
        in_specs=[pl.BlockSpec((None, TS, RWKV_COLS), lambda b, t: (b, t, 0)),
                  pl.BlockSpec((None, SUBLANES, RWKV_COLS),
                               lambda b, t: (b, jnp.maximum(t * halo_blocks - 1, 0), 0))]
                 + [_const_spec(c) for c in consts],
        out_specs=pl.BlockSpec((None, TS, RWKV_DIM), lambda b, t: (b, t, 0)),
        out_shape=jax.ShapeDtypeStruct((B, S, RWKV_DIM), BF16),
        scratch_shapes=[pltpu.VMEM((RWKV_HEADS // 2, 2 * CHUNK, 2 * CHUNK), F32)] + [stage] * 9,
        compiler_params=_params("parallel", "arbitrary"),
        name="rwkv",
    )(zr, zr, *consts)


Q_SLAB, K_SLAB, V_SLAB, N_SLABS = 0, 4, 6, 10


def _fold_rows(tile, sub, dil):
    return pl.ds(tile * (ATTN_ROWS // dil) + sub, dil, stride=FOLD_PITCH)


def _attn_group(gi, dil, S, x_s, o_s, l_s, eye_ref, bias_ref):
    L = S // dil
    nb = L // BAND_BLOCK
    Q = BAND_BLOCK
    heads = range(HEADS_PER_GROUP)
    lane2 = lax.broadcasted_iota(jnp.int32, (Q, 2 * LANES), 1)
    hmask = [lane2 // HEAD_DIM == j for j in heads]
    eye = eye_ref[...]

    def load(slab, rows):
        return x_s[slab, rows, :].astype(BF16)

    units = [divmod(u, nb) for u in range(dil * nb)]
    cur, q, k_aug, v = [], [], [], []
    for res, blk in units:
        if nb > 1:
            rows = pl.ds(res + blk * (Q * dil), Q, stride=dil)
        else:
            rows = pl.ds(res * FOLD_PITCH, Q)
        if blk > 0:
            prev = pl.ds(res + (blk - 1) * (Q * dil), Q, stride=dil)
            both = lambda slab, prev=prev, rows=rows: jnp.concatenate([load(slab, prev), load(slab, rows)], axis=0)
            bias = bias_ref[...]
        else:
            both = lambda slab, rows=rows: load(slab, rows)
            bias = bias_ref[Q:, :]
        cur.append(rows)
        q.append([jnp.concatenate([load(Q_SLAB + j, rows), eye], axis=1) for j in heads])
        k_aug.append([jnp.concatenate([both(K_SLAB + hp), bias], axis=1) for hp in range(2)])
        v.append([both(V_SLAB + j) for j in heads])
    chains = [(n, j) for n in range(len(units)) for j in heads]
    s = [_dot_nt(q[n][j], k_aug[n][j // 2]) for n, j in chains]
    m = [jnp.max(x, axis=-1, keepdims=True) for x in s]
    e = [jnp.exp2(x - y) for x, y in zip(s, m)]
    l = [jnp.sum(x, axis=-1, keepdims=True) for x in e]
    p = [(x * (1.0 / y)).astype(BF16) for x, y in zip(e, l)]
    for n in range(len(units)):
        c = n * HEADS_PER_GROUP
        lse = jnp.broadcast_to(m[c] * LN2 + jnp.log(l[c]), (Q, 2 * LANES))
        for j in heads[1:]:
            lse = jnp.where(hmask[j], m[c + j] * LN2 + jnp.log(l[c + j]), lse)
        for hp in range(2):
            pp = jnp.concatenate([p[c + 2 * hp], p[c + 2 * hp + 1]], axis=1)
            vv = jnp.concatenate([v[n][2 * hp], v[n][2 * hp + 1]], axis=0)
            o_s[gi, hp, cur[n], :] = _dot(pp, vv)
            l_s[gi, hp, cur[n], :] = lse[:, hp * LANES:(hp + 1) * LANES]


def _attn_kernel(q_ref, k_ref, v_ref, pos_ref, invf_ref, qg_ref, qgp_ref, kg_ref, kgp_ref, seg_ref, rot_ref,
                 eye_ref, bias_ref, spread_ref, out_ref, x_s, o_s, l_s, cs_s):
    g = pl.program_id(1)
    S = q_ref.shape[0]
    R = ATTN_ROWS
    ng = len(ATTN_GROUPS)
    fold_dil = ATTN_GROUPS[-1][1]

    @pl.when(g == 0)
    def _():
        rc = R

        def body(i, carry):
            ang = pos_ref[pl.ds(pl.multiple_of(i * rc, rc), rc), :].astype(F32) * invf_ref[...]
            for tbl, fn in enumerate((jnp.cos, jnp.sin)):
                val = fn(ang)
                hi = val.astype(BF16)
                rest = val - hi.astype(F32)
                mid = rest.astype(BF16)
                lo = (rest - mid.astype(F32)).astype(BF16)
                q_gain, k_gain = ((qg_ref, kg_ref), (qgp_ref, kgp_ref))[tbl]
                for sub in range(POS_PER_ROW):
                    spread = spread_ref[sub]
                    full = _dot(hi, spread) + _dot(mid, spread) + _dot(lo, spread)
                    rows = pl.ds(i * (rc * POS_PER_ROW) + sub, rc, stride=POS_PER_ROW)
                    cs_s[tbl, rows, :] = full * (q_gain[...] * (HEAD_DIM ** -0.5 * LOG2E))
                    cs_s[2 + tbl, rows, :] = full * k_gain[...]
            return carry
        lax.fori_loop(0, S // (rc * POS_PER_ROW), body, 0)

    seg = seg_ref[...]
    rot = rot_ref[...]
    first = lax.broadcasted_iota(jnp.int32, (R, LANES), 1) < HEAD_DIM

    def prep(folded, i, carry):
        rows = pl.ds(pl.multiple_of(i * R, R), R)

        def store(slab, val):
            if folded:
                for sub in range(R // fold_dil):
                    x_s[slab, _fold_rows(i, sub, fold_dil), :] = val[sub * fold_dil:(sub + 1) * fold_dil]
            else:
                x_s[slab, rows, :] = val

        def store_per_head(slab, val):
            store(slab, jnp.where(first, val, 0.0))
            store(slab + 1, jnp.where(first, 0.0, val))

        def normed(src, hf, tables):
            x = src[rows, hf * LANES:(hf + 1) * LANES]
            xf = x.astype(F32)
            inv = lax.rsqrt(_segsum(xf * xf, seg) + RMS_EPS)
            xr = _dot(x, rot)
            return (xf * cs_s[tables, rows, :] + xr * cs_s[tables + 1, rows, :]) * inv

        for hf in range(2):
            store_per_head(Q_SLAB + 2 * hf, normed(q_ref, hf, 0))
            store(K_SLAB + hf, normed(k_ref, hf, 2))
            store_per_head(V_SLAB + 2 * hf, v_ref[rows, hf * LANES:(hf + 1) * LANES].astype(F32))
        return carry

    @pl.when(g < ng - 1)
    def _():
        lax.fori_loop(0, S // R, functools.partial(prep, False), 0)

    @pl.when(g == ng - 1)
    def _():
        lax.fori_loop(0, S // R, functools.partial(prep, True), 0)

    for gi, (_, dil) in enumerate(ATTN_GROUPS):
        pl.when(g == gi)(functools.partial(_attn_group, gi, dil, S, x_s, o_s, l_s, eye_ref, bias_ref))

    @pl.when(g == ng - 1)
    def _():
        def body(i, carry):
            rows = pl.ds(pl.multiple_of(i * R, R), R)

            def fetch(ref, gi, slab):
                if gi < ng - 1:
                    return ref[gi, slab, rows, :]
                return jnp.concatenate([ref[gi, slab, _fold_rows(i, sub, fold_dil), :]
                                        for sub in range(R // fold_dil)], axis=0)

            for slab in range(2):
                ls = [fetch(l_s, gi, slab) for gi in range(ng)]
                m = jnp.maximum(jnp.maximum(ls[0], ls[1]), ls[2])
                ws = [jnp.exp(x - m) for x in ls]
                num = sum(w * fetch(o_s, gi, slab) for gi, w in enumerate(ws))
                out_ref[rows, slab * LANES:(slab + 1) * LANES] = (num / (ws[0] + ws[1] + ws[2])).astype(BF16)
            return carry
        lax.fori_loop(0, S // R, body, 0)


def _attn(za, pos, consts, B, S):
    ng = len(ATTN_GROUPS)
    gw = ATTN_OUT_DIM
    rows = (S // BAND_BLOCK) * FOLD_PITCH
    return pl.pallas_call(
        _attn_kernel,
        grid=(B, ng),
        in_specs=[pl.BlockSpec((None, S, gw), lambda b, g: (b, 0, g)),
                  pl.BlockSpec((None, S, gw), lambda b, g: (b, 0, ng + g)),
                  pl.BlockSpec((None, S, gw), lambda b, g: (b, 0, 2 * ng + g)),
                  pl.BlockSpec((None, S // POS_PER_ROW, LANES), lambda b, g: (b, 0, 0))]
                 + [_const_spec(c) for c in consts],
        out_specs=pl.BlockSpec((None, S, ATTN_OUT_DIM), lambda b, g: (b, 0, 0)),
        out_shape=jax.ShapeDtypeStruct((B, S, ATTN_OUT_DIM), BF16),
        scratch_shapes=[pltpu.VMEM((N_SLABS, rows, LANES), F32), pltpu.VMEM((ng, 2, rows, LANES), F32),
                        pltpu.VMEM((ng, 2, rows, LANES), F32), pltpu.VMEM((4, S, LANES), F32)],
        compiler_params=_params("parallel", "arbitrary"),
        name="attn",
    )(za, za, za, pos, *consts)


def _out_kernel(x1_ref, gt_ref, yr_ref, ya_ref, p_ref, wbr_ref, wba_ref, wo_ref, n2_ref, wg_ref, wu_ref,
                wd_ref, pn_ref, pwg_ref, pwp_ref, out_ref):
    br = _dot(yr_ref[...], wbr_ref[...])
    ba = _dot(ya_ref[...], wba_ref[...])
    merged = gt_ref[:, :D_MODEL].astype(F32) * br + gt_ref[:, D_MODEL:].astype(F32) * ba
    x2 = x1_ref[...] + _dot(merged.astype(BF16), wo_ref[...])
    h = _rms(x2, n2_ref[...]).astype(BF16)
    g = _dot(h, wg_ref[...])
    u = _dot(h, wu_ref[...])
    act = (g * _sigmoid(g) * u).astype(BF16)
    x3 = x2 + 0.5 * _dot(act, wd_ref[...])
    hp = _rms(x3, pn_ref[...]).astype(BF16)
    gate = _sigmoid(_dot(hp, pwg_ref[...]))
    out_ref[...] = x3 + gate * _dot(p_ref[...].astype(BF16), pwp_ref[...])


def _out(x1, gt, yr, ya, p2d, consts):
    T = x1.shape[0]
    TM = OUT_TM
    row = lambda i: (i, 0)
    return pl.pallas_call(
        _out_kernel,
        grid=(T // TM,),
        in_specs=[pl.BlockSpec((TM, D_MODEL), row), pl.BlockSpec((TM, GATE_COLS), row),
                  pl.BlockSpec((TM, RWKV_DIM), row), pl.BlockSpec((TM, ATTN_OUT_DIM), row),
                  pl.BlockSpec((TM, PLE_DIM), row)] + [_const_spec(c) for c in consts],
        out_specs=pl.BlockSpec((TM, D_MODEL), row),
        out_shape=jax.ShapeDtypeStruct((T, D_MODEL), F32),
        compiler_params=_params("parallel"),
        name="merge_ffn2",
    )(x1, gt, yr, ya, p2d, *consts)


def _rotate_half_matrix():
    half = HEAD_DIM // 2
    m = np.zeros((LANES, LANES), np.float32)
    for lane in range(LANES):
        if lane % HEAD_DIM < half:
            m[lane + half, lane] = -1.0
        else:
            m[lane - half, lane] = 1.0
    return jnp.asarray(m, BF16)


def _band_bias():
    kq = np.arange(BAND_BLOCK)[:, None]
    qq = np.arange(BAND_BLOCK)[None, :]
    cur = np.where(kq <= qq, 0.0, NEG_INF)
    prev = np.where(kq >= qq, 0.0, NEG_INF)
    return jnp.asarray(np.concatenate([prev, cur]), BF16)


def _block_ones(n, blk):
    idx = np.arange(n) // blk
    return jnp.asarray(idx[:, None] == idx[None, :], BF16)


def kernel(x, p, positions, ffn1_norm, ffn1_w_gate, ffn1_w_up, ffn1_w_down, mix_norm, w_in, rwkv_mu, rwkv_w0, rwkv_w2, rwkv_a0, rwkv_a2, rwkv_g2, rwkv_k_k, rwkv_k_a, rwkv_r_k, rwkv_gn_w, rwkv_gn_b, q_norm, k_norm, w_br_rwkv, w_br_attn, w_out, ffn2_norm, ffn2_w_gate, ffn2_w_up, ffn2_w_down, ple_norm, ple_w_gate, ple_w_proj):
    B, S, D = x.shape
    depth = p.shape[0]
    T = B * S
    half = HEAD_DIM // 2
    bf = lambda w: w.astype(BF16)
    rowvec = lambda a: a.reshape(1, -1).astype(F32)

    inv_freq = 1.0 / (ROPE_THETA ** (jnp.arange(0, HEAD_DIM, 2, dtype=F32) / HEAD_DIM))
    invf = jnp.tile(inv_freq, HEADS_PER_GROUP).reshape(1, LANES)
    pos = jnp.repeat(positions.reshape(B, S // POS_PER_ROW, POS_PER_ROW), half, axis=2)
    lane = np.arange(LANES)
    spread = jnp.asarray(np.stack([lane[:, None] == sub * half + lane[None, :] % half
                                   for sub in range(POS_PER_ROW)]), BF16)
    seg64 = _block_ones(2 * LANES, HEAD_DIM)
    seg_mean = _block_ones(LANES, HEAD_DIM) * (1.0 / HEAD_DIM)
    rot = _rotate_half_matrix()
    eye = jnp.eye(BAND_BLOCK, dtype=BF16)
    heads_per_slab = LANES // HEAD_DIM
    gain = lambda a: jnp.tile(a, heads_per_slab).reshape(1, LANES).astype(F32)
    partner = lambda a: gain(jnp.concatenate([a[half:], a[:half]]))

    xc = x.reshape(T, D)
    for i in range(depth):
        zeros = jnp.zeros((DECAY_LORA, RWKV_DIM), F32)
        w2a = jnp.concatenate([jnp.concatenate([rwkv_w2[i], zeros], axis=1),
                               jnp.concatenate([zeros, rwkv_a2[i]], axis=1)], axis=0)

        x1, h2, w_in_bf = _ffn1(xc, rowvec(ffn1_norm[i]), bf(ffn1_w_gate[i]), bf(ffn1_w_up[i]), bf(ffn1_w_down[i]),
                                rowvec(mix_norm[i]), [w_in[i]])
        out_weights = [w_br_rwkv[i], w_br_attn[i], w_out[i], ffn2_w_gate[i], ffn2_w_up[i], ffn2_w_down[i],
                       ple_w_gate[i], ple_w_proj[i]]
        zr, za, gt, wbr, wba, wo, wg2, wu2, wd2, pwg, pwp = _inproj(h2, w_in_bf, out_weights)
        yr = _rwkv(zr.reshape(B, S, RWKV_COLS), rowvec(rwkv_mu[i]), rowvec(rwkv_w0[i]), rowvec(rwkv_a0[i]),
                   bf(w2a), bf(rwkv_g2[i]), rowvec(rwkv_k_k[i]), rowvec(rwkv_k_a[i]), rowvec(rwkv_r_k[i]),
                   rowvec(rwkv_gn_w[i]), rowvec(rwkv_gn_b[i]), seg64, B, S)
        attn_consts = [invf, gain(q_norm[i]), partner(q_norm[i]), gain(k_norm[i]), partner(k_norm[i]),
                       seg_mean, rot, eye, _band_bias(), spread]
        ya = _attn(za.reshape(B, S, ATTN_COLS), pos, attn_consts, B, S)
        consts = [wbr, wba, wo, rowvec(ffn2_norm[i]), wg2, wu2, wd2, rowvec(ple_norm[i]), pwg, pwp]
        xc = _out(x1, gt, yr.reshape(T, RWKV_DIM), ya.reshape(T, ATTN_OUT_DIM), p[i].reshape(T, PLE_DIM), consts)
    return xc.reshape(B, S, D)
```

```python
import functools
import math

import numpy as np
import jax
import jax.numpy as jnp
from jax import lax
from jax.experimental import pallas as pl
from jax.experimental.pallas import tpu as pltpu

F32 = jnp.float32
BF16 = jnp.bfloat16

D_MODEL = 1024
PLE_DIM = 256
HEAD_DIM = 64
RWKV_HEADS = 8
RWKV_DIM = RWKV_HEADS * HEAD_DIM
DECAY_LORA = 64
ICLR_LORA = 64
GATE_LORA = 128
GN_EPS = 64e-5
ATTN_GROUPS = ((128, 1), (512, 4), (2048, 16))
HEADS_PER_GROUP = 4
ATTN_DIM = HEADS_PER_GROUP * len(ATTN_GROUPS) * HEAD_DIM
ATTN_OUT_DIM = HEADS_PER_GROUP * HEAD_DIM
BAND_BLOCK = 128
ROPE_THETA = 10000.0
NEG_INF = -1e30
RMS_EPS = 1e-6
LOG2E = math.log2(math.e)
LN2 = math.log(2.0)
RWKV_COLS = 3 * RWKV_DIM + DECAY_LORA + ICLR_LORA + GATE_LORA
ATTN_COLS = 3 * ATTN_DIM
GATE_COLS = 2 * D_MODEL

LANES = 128
SUBLANES = 8
BF16_ROWS = 16
VMEM_LIMIT_BYTES = 60 * 1024 * 1024

FFN_TM = 512
PROJ_TM = 1024
OUT_TM = 512
RWKV_TS = 512
CHUNK = 64
ROW_PARTS = 4
ATTN_ROWS = 512
FOLD_PITCH = BAND_BLOCK + SUBLANES
POS_PER_ROW = LANES // (HEAD_DIM // 2)


def _dot(a, b):
    return jnp.dot(a, b, preferred_element_type=F32)


def _dot_nt(a, b):
    return lax.dot_general(a, b, (((1,), (1,)), ((), ())), preferred_element_type=F32)


def _dot_tn(a, b):
    return lax.dot_general(a, b, (((0,), (0,)), ((), ())), preferred_element_type=F32)


def _rms(x, gain):
    return x * lax.rsqrt(jnp.mean(x * x, axis=-1, keepdims=True) + RMS_EPS) * gain


def _sigmoid(x):
    return 1.0 / (1.0 + jnp.exp(-x))


def _segsum(x, seg):
    w = seg.shape[0]
    parts = [_dot(x[:, lo:lo + w].astype(BF16), seg) for lo in range(0, x.shape[1], w)]
    return parts[0] if len(parts) == 1 else jnp.concatenate(parts, axis=1)


def _const_spec(arr):
    nd = arr.ndim
    return pl.BlockSpec(arr.shape, lambda *_: (0,) * nd, pipeline_mode=pl.Buffered(1))


def _params(*sem):
    return pltpu.CompilerParams(dimension_semantics=sem, vmem_limit_bytes=VMEM_LIMIT_BYTES)


def _row_parts(n):
    return [slice(i * n // ROW_PARTS, (i + 1) * n // ROW_PARTS) for i in range(ROW_PARTS)]


def _cast_plan(weights, nsteps):
    specs, shapes = [], []
    for w in weights:
        rows, cols = w.shape
        every = 1
        while (rows * every) % nsteps or (rows * every // nsteps) % BF16_ROWS:
            every *= 2
        specs.append(pl.BlockSpec((rows * every // nsteps, cols), lambda i, every=every: (i // every, 0)))
        shapes.append(jax.ShapeDtypeStruct(w.shape, BF16))
    return specs, shapes


def _cast_slabs(src_refs, dst_refs):
    for src, dst in zip(src_refs, dst_refs):
        dst[...] = src[...].astype(BF16)


def _ffn1_kernel(ncast, x_ref, n1_ref, wg_ref, wu_ref, wd_ref, n2_ref, *rest):
    cast_in, (x1_ref, h2_ref), cast_out = rest[:ncast], rest[ncast:ncast + 2], rest[ncast + 2:]
    _cast_slabs(cast_in, cast_out)
    for rows in _row_parts(x_ref.shape[0]):
        x = x_ref[rows, :]
        h = _rms(x, n1_ref[...]).astype(BF16)
        g = _dot(h, wg_ref[...])
        u = _dot(h, wu_ref[...])
        act = (g * _sigmoid(g) * u).astype(BF16)
        x1 = x + 0.5 * _dot(act, wd_ref[...])
        x1_ref[rows, :] = x1
        h2_ref[rows, :] = _rms(x1, n2_ref[...]).astype(BF16)


def _ffn1(x2d, n1, wg, wu, wd, n2, next_weights):
    T = x2d.shape[0]
    nsteps = T // FFN_TM
    row = lambda i: (i, 0)
    cast_specs, cast_shapes = _cast_plan(next_weights, nsteps)
    return pl.pallas_call(
        functools.partial(_ffn1_kernel, len(next_weights)),
        grid=(nsteps,),
        in_specs=[pl.BlockSpec((FFN_TM, D_MODEL), row), _const_spec(n1), _const_spec(wg),
                  _const_spec(wu), _const_spec(wd), _const_spec(n2)] + cast_specs,
        out_specs=[pl.BlockSpec((FFN_TM, D_MODEL), row), pl.BlockSpec((FFN_TM, D_MODEL), row)] + cast_specs,
        out_shape=[jax.ShapeDtypeStruct((T, D_MODEL), F32), jax.ShapeDtypeStruct((T, D_MODEL), BF16)]
                  + cast_shapes,
        compiler_params=_params("arbitrary"),
        name="ffn1",
    )(x2d, n1, wg, wu, wd, n2, *next_weights)


def _inproj_kernel(ncast, h_ref, w_ref, *rest):
    cast_in, (zr_ref, za_ref, gt_ref), cast_out = rest[:ncast], rest[ncast:ncast + 3], rest[ncast + 3:]
    _cast_slabs(cast_in, cast_out)
    for rows in _row_parts(h_ref.shape[0]):
        h = h_ref[rows, :]
        zr_ref[rows, :] = _dot(h, w_ref[:, :RWKV_COLS])
        za_ref[rows, :] = _dot(h, w_ref[:, RWKV_COLS:RWKV_COLS + ATTN_COLS]).astype(BF16)
        gt_ref[rows, :] = _sigmoid(_dot(h, w_ref[:, RWKV_COLS + ATTN_COLS:])).astype(BF16)


def _inproj(h2, w, next_weights):
    T = h2.shape[0]
    nsteps = T // PROJ_TM
    row = lambda i: (i, 0)
    cast_specs, cast_shapes = _cast_plan(next_weights, nsteps)
    return pl.pallas_call(
        functools.partial(_inproj_kernel, len(next_weights)),
        grid=(nsteps,),
        in_specs=[pl.BlockSpec((PROJ_TM, D_MODEL), row), _const_spec(w)] + cast_specs,
        out_specs=[pl.BlockSpec((PROJ_TM, RWKV_COLS), row), pl.BlockSpec((PROJ_TM, ATTN_COLS), row),
                   pl.BlockSpec((PROJ_TM, GATE_COLS), row)] + cast_specs,
        out_shape=[jax.ShapeDtypeStruct((T, RWKV_COLS), F32), jax.ShapeDtypeStruct((T, ATTN_COLS), BF16),
                   jax.ShapeDtypeStruct((T, GATE_COLS), BF16)] + cast_shapes,
        compiler_params=_params("arbitrary"),
        name="inproj",
    )(h2, w, *next_weights)


def _scan_chunks(row_list, h_ref, r_s, k_s, v_s, na_s, b_s, lw_s, cum_s, y_s):
    C = CHUNK
    C2 = 2 * C
    pairs = range(RWKV_HEADS // 2)
    rows_of = [rows for rows in row_list for _ in pairs]
    lanes = [slice(pr * LANES, (pr + 1) * LANES) for _ in row_list for pr in pairs]
    ci = lax.broadcasted_iota(jnp.int32, (C, LANES), 1)
    ri = lax.broadcasted_iota(jnp.int32, (C, LANES), 0)
    first = ci < HEAD_DIM
    strict = ci % C < ri
    incl = ci % C <= ri
    eye = jnp.where(ci % C == ri, 1.0, 0.0)
    ri2 = lax.broadcasted_iota(jnp.int32, (C2, 2 * C2), 0)
    ci2 = lax.broadcasted_iota(jnp.int32, (C2, 2 * C2), 1)
    same_head = ri2 // C == (ci2 % C2) // C
    diag = lax.broadcasted_iota(jnp.int32, (C2, C2), 0) == lax.broadcasted_iota(jnp.int32, (C2, C2), 1)

    def stack(x):
        return jnp.concatenate([jnp.where(first, x, 0.0), jnp.where(first, 0.0, x)], axis=0)

    def cat(parts, axis):
        return jnp.concatenate(parts, axis=axis)

    def bf(x):
        return x.astype(BF16)

    def swap(x):
        return pltpu.roll(x, HEAD_DIM, 1)

    def load(ref):
        return [ref[rows, ln] for rows, ln in zip(rows_of, lanes)]

    lw = load(lw_s)
    cum = load(cum_s)
    p_inc = [jnp.exp(x) for x in cum]
    p_exc = [jnp.exp(x - y) for x, y in zip(cum, lw)]
    p_inv = [jnp.exp(-x) for x in cum]
    p_end = [x[C - 1:C, :] for x in p_inc]
    to_end = [x * y for x, y in zip(p_inv, p_end)]

    a_st = [stack(x * p) for x, p in zip(load(na_s), p_exc)]
    r_pk = [x * p for x, p in zip(load(r_s), p_inc)]
    v_pk = load(v_s)
    v_st = [bf(stack(x)) for x in v_pk]
    b = load(b_s)
    k = load(k_s)
    xr = [bf(cat([a, stack(r)], 0)) for a, r in zip(a_st, r_pk)]
    y_g = [bf(cat([bb * p, kk * p], 0)) for bb, kk, p in zip(b, k, p_inv)]
    y_h = [bf(cat([bb * p, kk * p], 0)) for bb, kk, p in zip(b, k, to_end)]

    g = [_dot_nt(x, y) for x, y in zip(xr, y_g)]
    l_ab = [jnp.where(strict & first, x[:C], jnp.where(strict, swap(x[C:C2]), 0.0)) for x in g]
    l_ak = [jnp.where(strict & first, swap(x[:C]), jnp.where(strict, x[C:C2], 0.0)) for x in g]
    m_rb = [jnp.where(incl & first, x[C2:C2 + C], jnp.where(incl, swap(x[C2 + C:]), 0.0)) for x in g]
    m_rk = [jnp.where(incl & first, swap(x[C2:C2 + C]), jnp.where(incl, x[C2 + C:], 0.0)) for x in g]

    pw = [_dot(bf(x), bf(stack(x))) for x in l_ab]
    acc = [eye + x for x in l_ab]
    for _ in range(1, int(math.log2(C)) - 1):
        tp = [_dot(bf(cat([t, p], 0)), bf(stack(p))) for t, p in zip(acc, pw)]
        acc = [t + x[:C] for t, x in zip(acc, tp)]
        pw = [x[C:] for x in tp]
    acc = [t + _dot(bf(t), bf(stack(p))) for t, p in zip(acc, pw)]

    lm = [_dot(bf(cat([x, y], 0)), vv) for x, y, vv in zip(l_ak, m_rk, v_st)]
    av = [_dot(bf(t), bf(cat([a, stack(x[:C])], 1))) for t, a, x in zip(acc, a_st, lm)]
    ry = [_dot(bf(m), bf(cat([stack(x[:, :LANES]), stack(x[:, LANES:])], 1))) + cat([r, y[C:]], 1)
          for m, x, r, y in zip(m_rb, av, r_pk, lm)]
    zero = jnp.zeros((C, LANES), F32)
    md = [jnp.where(same_head, _dot_tn(yh, bf(cat([x, cat([zero, vv], 1)], 0))), 0.0)
          for yh, x, vv in zip(y_h, av, v_pk)]
    r_new = [bf(y[:, :LANES]) for y in ry]
    m_new = [bf(jnp.where(diag, jnp.broadcast_to(pe, (C2, C2)), 0.0) + x[:, :C2])
             for pe, x in zip(p_end, md)]

    h = [h_ref[pr] for pr in pairs]
    for ci_, rows in enumerate(row_list):
        base = ci_ * len(pairs)
        hb = [bf(x) for x in h]
        yy = [_dot(r_new[base + pr], hb[pr]) for pr in pairs]
        h = [_dot(m_new[base + pr], hb[pr]) + md[base + pr][:, C2:] for pr in pairs]
        for pr in pairs:
            y_s[rows, lanes[pr]] = yy[pr] + ry[base + pr][:, LANES:]
    for pr in pairs:
        h_ref[pr] = h[pr]


def _rwkv_kernel(z_ref, zh_ref, mu_ref, w0_ref, a0_ref, w2a_ref, g2_ref, kk_ref, ka_ref, rk_ref,
                 gnw_ref, gnb_ref, seg_ref, tril_ref, y_ref,
                 h_ref, r_s, k_s, v_s, na_s, b_s, lw_s, cum_s, g_s, y_s):
    t = pl.program_id(1)
    TS = RWKV_TS
    D = RWKV_DIM

    @pl.when(t == 0)
    def _():
        h_ref[...] = jnp.zeros_like(h_ref)

    has_prev = jnp.where(t > 0, 1.0, 0.0).astype(F32)

    def shifted(lo, hi):
        zc = z_ref[:, lo:hi]
        prev = zh_ref[SUBLANES - 1:SUBLANES, lo:hi] * has_prev
        rolled = pltpu.roll(zc, 1, 0)
        row = lax.broadcasted_iota(jnp.int32, (SUBLANES, hi - lo), 0)
        zp = jnp.concatenate([jnp.where(row == 0, prev, rolled[:SUBLANES]), rolled[SUBLANES:]], axis=0)
        return zc + (zp - zc) * mu_ref[:, lo:hi]

    seg = seg_ref[...]
    r = shifted(0, D)
    k = shifted(D, 2 * D)
    v = shifted(2 * D, 3 * D)
    wa = shifted(3 * D, 3 * D + DECAY_LORA + ICLR_LORA)
    gd = shifted(3 * D + DECAY_LORA + ICLR_LORA, RWKV_COLS)

    lane = lax.broadcasted_iota(jnp.int32, wa.shape, 1)
    lora_in = jnp.where(lane < DECAY_LORA, jnp.tanh(wa), wa).astype(BF16)
    lora = _dot(lora_in, w2a_ref[...])
    lw = -math.exp(-0.5) * _sigmoid(w0_ref[...] + lora[:, :D])
    lw_s[...] = lw
    tril = tril_ref[...]
    hi = lw.astype(BF16)
    lo = (lw - hi.astype(F32)).astype(BF16)
    span = tril.shape[0]
    for r0 in range(0, TS, span):
        cum_s[r0:r0 + span, :] = _dot(tril, hi[r0:r0 + span]) + _dot(tril, lo[r0:r0 + span])
    a = _sigmoid(a0_ref[...] + lora[:, D:])
    g_s[...] = _dot(_sigmoid(gd).astype(BF16), g2_ref[...])

    kk = k * kk_ref[...]
    kk = kk * lax.rsqrt(jnp.maximum(_segsum(kk * kk, seg), 1e-24))
    k2 = k * (1.0 + (a - 1.0) * ka_ref[...])
    r_s[...] = r
    k_s[...] = k2
    v_s[...] = v
    na_s[...] = -kk
    b_s[...] = kk * a

    row_list = [slice(c * CHUNK, (c + 1) * CHUNK) for c in range(TS // CHUNK)]
    _scan_chunks(row_list, h_ref, r_s, k_s, v_s, na_s, b_s, lw_s, cum_s, y_s)

    y = y_s[...]
    inv_n = 1.0 / HEAD_DIM
    mean = _segsum(y, seg) * inv_n
    yc = y - mean
    var = _segsum(yc * yc, seg) * inv_n
    yn = yc * lax.rsqrt(var + GN_EPS) * gnw_ref[...] + gnb_ref[...]
    bonus = _segsum(r_s[...] * k_s[...] * rk_ref[...], seg) * v_s[...]
    y_ref[...] = ((yn + bonus) * g_s[...]).astype(BF16)


def _rwkv(zr, mu, w0, a0, w2a, g2, k_k, k_a, r_k, gn_w, gn_b, seg, B, S):
    TS = RWKV_TS
    t_idx = np.arange(2 * CHUNK)
    tril = jnp.asarray((t_idx[:, None] // CHUNK == t_idx[None, :] // CHUNK)
                       & (t_idx[None, :] <= t_idx[:, None]), BF16)
    consts = [mu, w0, a0, w2a, g2, k_k, k_a, r_k, gn_w, gn_b, seg, tril]
    halo_blocks = TS // SUBLANES
    stage = pltpu.VMEM((TS, RWKV_DIM), F32)
    return pl.pallas_call(
        _rwkv_kernel,
        grid=(B, S // TS),
        in_specs=[pl.BlockSpec((None, TS, RWKV_COLS), lambda b, t: (b, t, 0)),
                  pl.BlockSpec((None, SUBLANES, RWKV_COLS),
                               lambda b, t: (b, jnp.maximum(t * halo_blocks - 1, 0), 0))]
                 + [_const_spec(c) for c in consts],
        out_specs=pl.BlockSpec((None, TS, RWKV_DIM), lambda b, t: (b, t, 0)),
        out_shape=jax.ShapeDtypeStruct((B, S, RWKV_DIM), BF16),
        scratch_shapes=[pltpu.VMEM((RWKV_HEADS // 2, 2 * CHUNK, 2 * CHUNK), F32)] + [stage] * 9,
        compiler_params=_params("parallel", "arbitrary"),
        name="rwkv",
    )(zr, zr, *consts)


Q_SLAB, K_SLAB, V_SLAB, N_SLABS = 0, 4, 6, 10


def _fold_rows(tile, sub, dil):
    return pl.ds(tile * (ATTN_ROWS // dil) + sub, dil, stride=FOLD_PITCH)


def _attn_group(gi, dil, S, x_s, o_s, l_s, eye_ref, bias_ref):
    L = S // dil
    nb = L // BAND_BLOCK
    Q = BAND_BLOCK
    heads = range(HEADS_PER_GROUP)
    lane2 = lax.broadcasted_iota(jnp.int32, (Q, 2 * LANES), 1)
    hmask = [lane2 // HEAD_DIM == j for j in heads]
    eye = eye_ref[...]

    def load(slab, rows):
        return x_s[slab, rows, :].astype(BF16)

    units = [divmod(u, nb) for u in range(dil * nb)]
    cur, q, k_aug, v = [], [], [], []
    for res, blk in units:
        if nb > 1:
            rows = pl.ds(res + blk * (Q * dil), Q, stride=dil)
        else:
            rows = pl.ds(res * FOLD_PITCH, Q)
        if blk > 0:
            prev = pl.ds(res + (blk - 1) * (Q * dil), Q, stride=dil)
            both = lambda slab, prev=prev, rows=rows: jnp.concatenate([load(slab, prev), load(slab, rows)], axis=0)
            bias = bias_ref[...]
        else:
            both = lambda slab, rows=rows: load(slab, rows)
            bias = bias_ref[Q:, :]
        cur.append(rows)
        q.append([jnp.concatenate([load(Q_SLAB + j, rows), eye], axis=1) for j in heads])
        k_aug.append([jnp.concatenate([both(K_SLAB + hp), bias], axis=1) for hp in range(2)])
        v.append([both(V_SLAB + j) for j in heads])
    chains = [(n, j) for n in range(len(units)) for j in heads]
    s = [_dot_nt(q[n][j], k_aug[n][j // 2]) for n, j in chains]
    m = [jnp.max(x, axis=-1, keepdims=True) for x in s]
    e = [jnp.exp2(x - y) for x, y in zip(s, m)]
    l = [jnp.sum(x, axis=-1, keepdims=True) for x in e]
    p = [x.astype(BF16) for x in e]
    inv = [1.0 / y for y in l]
    first = lax.broadcasted_iota(jnp.int32, (Q, LANES), 1) < HEAD_DIM
    for n in range(len(units)):
        c = n * HEADS_PER_GROUP
        lse = jnp.broadcast_to(m[c] * LN2 + jnp.log(l[c]), (Q, 2 * LANES))
        for j in heads[1:]:
            lse = jnp.where(hmask[j], m[c + j] * LN2 + jnp.log(l[c + j]), lse)
        for hp in range(2):
            pp = jnp.concatenate([p[c + 2 * hp], p[c + 2 * hp + 1]], axis=1)
            vv = jnp.concatenate([v[n][2 * hp], v[n][2 * hp + 1]], axis=0)
            o_s[gi, hp, cur[n], :] = _dot(pp, vv) * jnp.where(first, inv[c + 2 * hp], inv[c + 2 * hp + 1])
            l_s[gi, hp, cur[n], :] = lse[:, hp * LANES:(hp + 1) * LANES]


def _attn_kernel(q_ref, k_ref, v_ref, pos_ref, invf_ref, qg_ref, qgp_ref, kg_ref, kgp_ref, seg_ref, rot_ref,
                 eye_ref, bias_ref, spread_ref, out_ref, x_s, o_s, l_s, cs_s):
    g = pl.program_id(1)
    S = q_ref.shape[0]
    R = ATTN_ROWS
    ng = len(ATTN_GROUPS)
    fold_dil = ATTN_GROUPS[-1][1]

    @pl.when(g == 0)
    def _():
        rc = R

        def body(i, carry):
            ang = pos_ref[pl.ds(pl.multiple_of(i * rc, rc), rc), :].astype(F32) * invf_ref[...]
            for tbl, fn in enumerate((jnp.cos, jnp.sin)):
                val = fn(ang)
                hi = val.astype(BF16)
                rest = val - hi.astype(F32)
                mid = rest.astype(BF16)
                lo = (rest - mid.astype(F32)).astype(BF16)
                q_gain, k_gain = ((qg_ref, kg_ref), (qgp_ref, kgp_ref))[tbl]
                for sub in range(POS_PER_ROW):
                    spread = spread_ref[sub]
                    full = _dot(hi, spread) + _dot(mid, spread) + _dot(lo, spread)
                    rows = pl.ds(i * (rc * POS_PER_ROW) + sub, rc, stride=POS_PER_ROW)
                    cs_s[tbl, rows, :] = full * (q_gain[...] * (HEAD_DIM ** -0.5 * LOG2E))
                    cs_s[2 + tbl, rows, :] = full * k_gain[...]
            return carry
        lax.fori_loop(0, S // (rc * POS_PER_ROW), body, 0)

    seg = seg_ref[...]
    rot = rot_ref[...]
    first = lax.broadcasted_iota(jnp.int32, (R, LANES), 1) < HEAD_DIM

    def prep(folded, i, carry):
        rows = pl.ds(pl.multiple_of(i * R, R), R)

        def store(slab, val):
            if folded:
                for sub in range(R // fold_dil):
                    x_s[slab, _fold_rows(i, sub, fold_dil), :] = val[sub * fold_dil:(sub + 1) * fold_dil]
            else:
                x_s[slab, rows, :] = val

        def store_per_head(slab, val):
            store(slab, jnp.where(first, val, 0.0))
            store(slab + 1, jnp.where(first, 0.0, val))

        def normed(src, hf, tables):
            x = src[rows, hf * LANES:(hf + 1) * LANES]
            xf = x.astype(F32)
            inv = lax.rsqrt(_segsum(xf * xf, seg) + RMS_EPS)
            xr = _dot(x, rot)
            return (xf * cs_s[tables, rows, :] + xr * cs_s[tables + 1, rows, :]) * inv

        for hf in range(2):
            store_per_head(Q_SLAB + 2 * hf, normed(q_ref, hf, 0))
            store(K_SLAB + hf, normed(k_ref, hf, 2))
            store_per_head(V_SLAB + 2 * hf, v_ref[rows, hf * LANES:(hf + 1) * LANES].astype(F32))
        return carry

    @pl.when(g < ng - 1)
    def _():
        lax.fori_loop(0, S // R, functools.partial(prep, False), 0)

    @pl.when(g == ng - 1)
    def _():
        lax.fori_loop(0, S // R, functools.partial(prep, True), 0)

    for gi, (_, dil) in enumerate(ATTN_GROUPS):
        pl.when(g == gi)(functools.partial(_attn_group, gi, dil, S, x_s, o_s, l_s, eye_ref, bias_ref))

    @pl.when(g == ng - 1)
    def _():
        def body(i, carry):
            rows = pl.ds(pl.multiple_of(i * R, R), R)

            def fetch(ref, gi, slab):
                if gi < ng - 1:
                    return ref[gi, slab, rows, :]
                return jnp.concatenate([ref[gi, slab, _fold_rows(i, sub, fold_dil), :]
                                        for sub in range(R // fold_dil)], axis=0)

            for slab in range(2):
                ls = [fetch(l_s, gi, slab) for gi in range(ng)]
                m = jnp.maximum(jnp.maximum(ls[0], ls[1]), ls[2])
                ws = [jnp.exp(x - m) for x in ls]
                num = sum(w * fetch(o_s, gi, slab) for gi, w in enumerate(ws))
                out_ref[rows, slab * LANES:(slab + 1) * LANES] = (num / (ws[0] + ws[1] + ws[2])).astype(BF16)
            return carry
        lax.fori_loop(0, S // R, body, 0)


def _attn(za, pos, consts, B, S):
    ng = len(ATTN_GROUPS)
    gw = ATTN_OUT_DIM
    rows = (S // BAND_BLOCK) * FOLD_PITCH
    return pl.pallas_call(
        _attn_kernel,
        grid=(B, ng),
        in_specs=[pl.BlockSpec((None, S, gw), lambda b, g: (b, 0, g)),
                  pl.BlockSpec((None, S, gw), lambda b, g: (b, 0, ng + g)),
                  pl.BlockSpec((None, S, gw), lambda b, g: (b, 0, 2 * ng + g)),
                  pl.BlockSpec((None, S // POS_PER_ROW, LANES), lambda b, g: (b, 0, 0))]
                 + [_const_spec(c) for c in consts],
        out_specs=pl.BlockSpec((None, S, ATTN_OUT_DIM), lambda b, g: (b, 0, 0)),
        out_shape=jax.ShapeDtypeStruct((B, S, ATTN_OUT_DIM), BF16),
        scratch_shapes=[pltpu.VMEM((N_SLABS, rows, LANES), F32), pltpu.VMEM((ng, 2, rows, LANES), F32),
                        pltpu.VMEM((ng, 2, rows, LANES), F32), pltpu.VMEM((4, S, LANES), F32)],
        compiler_params=_params("parallel", "arbitrary"),
        name="attn",
    )(za, za, za, pos, *consts)


def _out_kernel(x1_ref, gt_ref, yr_ref, ya_ref, p_ref, wbr_ref, wba_ref, wo_ref, n2_ref, wg_ref, wu_ref,
                wd_ref, pn_ref, pwg_ref, pwp_ref, out_ref):
    br = _dot(yr_ref[...], wbr_ref[...])
    ba = _dot(ya_ref[...], wba_ref[...])
    merged = gt_ref[:, :D_MODEL].astype(F32) * br + gt_ref[:, D_MODEL:].astype(F32) * ba
    x2 = x1_ref[...] + _dot(merged.astype(BF16), wo_ref[...])
    h = _rms(x2, n2_ref[...]).astype(BF16)
    g = _dot(h, wg_ref[...])
    u = _dot(h, wu_ref[...])
    act = (g * _sigmoid(g) * u).astype(BF16)
    x3 = x2 + 0.5 * _dot(act, wd_ref[...])
    hp = _rms(x3, pn_ref[...]).astype(BF16)
    gate = _sigmoid(_dot(hp, pwg_ref[...]))
    out_ref[...] = x3 + gate * _dot(p_ref[...].astype(BF16), pwp_ref[...])


def _out(x1, gt, yr, ya, p2d, consts):
    T = x1.shape[0]
    TM = OUT_TM
    row = lambda i: (i, 0)
    return pl.pallas_call(
        _out_kernel,
        grid=(T // TM,),
        in_specs=[pl.BlockSpec((TM, D_MODEL), row), pl.BlockSpec((TM, GATE_COLS), row),
                  pl.BlockSpec((TM, RWKV_DIM), row), pl.BlockSpec((TM, ATTN_OUT_DIM), row),
                  pl.BlockSpec((TM, PLE_DIM), row)] + [_const_spec(c) for c in consts],
        out_specs=pl.BlockSpec((TM, D_MODEL), row),
        out_shape=jax.ShapeDtypeStruct((T, D_MODEL), F32),
        compiler_params=_params("parallel"),
        name="merge_ffn2",
    )(x1, gt, yr, ya, p2d, *consts)


def _rotate_half_matrix():
    half = HEAD_DIM // 2
    m = np.zeros((LANES, LANES), np.float32)
    for lane in range(LANES):
        if lane % HEAD_DIM < half:
            m[lane + half, lane] = -1.0
        else:
            m[lane - half, lane] = 1.0
    return jnp.asarray(m, BF16)


def _band_bias():
    kq = np.arange(BAND_BLOCK)[:, None]
    qq = np.arange(BAND_BLOCK)[None, :]
    cur = np.where(kq <= qq, 0.0, NEG_INF)
    prev = np.where(kq >= qq, 0.0, NEG_INF)
    return jnp.asarray(np.concatenate([prev, cur]), BF16)


def _block_ones(n, blk):
    idx = np.arange(n) // blk
    return jnp.asarray(idx[:, None] == idx[None, :], BF16)


def kernel(x, p, positions, ffn1_norm, ffn1_w_gate, ffn1_w_up, ffn1_w_down, mix_norm, w_in, rwkv_mu, rwkv_w0, rwkv_w2, rwkv_a0, rwkv_a2, rwkv_g2, rwkv_k_k, rwkv_k_a, rwkv_r_k, rwkv_gn_w, rwkv_gn_b, q_norm, k_norm, w_br_rwkv, w_br_attn, w_out, ffn2_norm, ffn2_w_gate, ffn2_w_up, ffn2_w_down, ple_norm, ple_w_gate, ple_w_proj):
    B, S, D = x.shape
    depth = p.shape[0]
    T = B * S
    half = HEAD_DIM // 2
    bf = lambda w: w.astype(BF16)
    rowvec = lambda a: a.reshape(1, -1).astype(F32)

    inv_freq = 1.0 / (ROPE_THETA ** (jnp.arange(0, HEAD_DIM, 2, dtype=F32) / HEAD_DIM))
    invf = jnp.tile(inv_freq, HEADS_PER_GROUP).reshape(1, LANES)
    pos = jnp.repeat(positions.reshape(B, S // POS_PER_ROW, POS_PER_ROW), half, axis=2)
    lane = np.arange(LANES)
    spread = jnp.asarray(np.stack([lane[:, None] == sub * half + lane[None, :] % half
                                   for sub in range(POS_PER_ROW)]), BF16)
    seg64 = _block_ones(2 * LANES, HEAD_DIM)
    seg_mean = _block_ones(LANES, HEAD_DIM) * (1.0 / HEAD_DIM)
    rot = _rotate_half_matrix()
    eye = jnp.eye(BAND_BLOCK, dtype=BF16)
    heads_per_slab = LANES // HEAD_DIM
    gain = lambda a: jnp.tile(a, heads_per_slab).reshape(1, LANES).astype(F32)
    partner = lambda a: gain(jnp.concatenate([a[half:], a[:half]]))

    xc = x.reshape(T, D)
    for i in range(depth):
        zeros = jnp.zeros((DECAY_LORA, RWKV_DIM), F32)
        w2a = jnp.concatenate([jnp.concatenate([rwkv_w2[i], zeros], axis=1),
                               jnp.concatenate([zeros, rwkv_a2[i]], axis=1)], axis=0)

        x1, h2, w_in_bf = _ffn1(xc, rowvec(ffn1_norm[i]), bf(ffn1_w_gate[i]), bf(ffn1_w_up[i]), bf(ffn1_w_down[i]),
                                rowvec(mix_norm[i]), [w_in[i]])
        out_weights = [w_br_rwkv[i], w_br_attn[i], w_out[i], ffn2_w_gate[i], ffn2_w_up[i], ffn2_w_down[i],
                       ple_w_gate[i], ple_w_proj[i]]
        zr, za, gt, wbr, wba, wo, wg2, wu2, wd2, pwg, pwp = _inproj(h2, w_in_bf, out_weights)
        yr = _rwkv(zr.reshape(B, S, RWKV_COLS), rowvec(rwkv_mu[i]), rowvec(rwkv_w0[i]), rowvec(rwkv_a0[i]),
                   bf(w2a), bf(rwkv_g2[i]), rowvec(rwkv_k_k[i]), rowvec(rwkv_k_a[i]), rowvec(rwkv_r_k[i]),
                   rowvec(rwkv_gn_w[i]), rowvec(rwkv_gn_b[i]), seg64, B, S)
        attn_consts = [invf, gain(q_norm[i]), partner(q_norm[i]), gain(k_norm[i]), partner(k_norm[i]),
                       seg_mean, rot, eye, _band_bias(), spread]
        ya = _attn(za.reshape(B, S, ATTN_COLS), pos, attn_consts, B, S)
        consts = [wbr, wba, wo, rowvec(ffn2_norm[i]), wg2, wu2, wd2, rowvec(ple_norm[i]), pwg, pwp]
        xc = _out(x1, gt, yr.reshape(T, RWKV_DIM), ya.reshape(T, ATTN_OUT_DIM), p[i].reshape(T, PLE_DIM), consts)
    return xc.reshape(B, S, D)
```

```python
import functools
import math

import numpy as np
import jax
import jax.numpy as jnp
from jax import lax
from jax.experimental import pallas as pl
from jax.experimental.pallas import tpu as pltpu

F32 = jnp.float32
BF16 = jnp.bfloat16

D_MODEL = 1024
PLE_DIM = 256
HEAD_DIM = 64
RWKV_HEADS = 8
RWKV_DIM = RWKV_HEADS * HEAD_DIM
DECAY_LORA = 64
ICLR_LORA = 64
GATE_LORA = 128
GN_EPS = 64e-5
ATTN_GROUPS = ((128, 1), (512, 4), (2048, 16))
HEADS_PER_GROUP = 4
ATTN_DIM = HEADS_PER_GROUP * len(ATTN_GROUPS) * HEAD_DIM
ATTN_OUT_DIM = HEADS_PER_GROUP * HEAD_DIM
BAND_BLOCK = 128
ROPE_THETA = 10000.0
NEG_INF = -1e30
RMS_EPS = 1e-6
LOG2E = math.log2(math.e)
LN2 = math.log(2.0)
RWKV_COLS = 3 * RWKV_DIM + DECAY_LORA + ICLR_LORA + GATE_LORA
ATTN_COLS = 3 * ATTN_DIM
GATE_COLS = 2 * D_MODEL

LANES = 128
SUBLANES = 8
BF16_ROWS = 16
VMEM_LIMIT_BYTES = 60 * 1024 * 1024

FFN_TM = 512
PROJ_TM = 1024
OUT_TM = 512
RWKV_TS = 1024
SCAN_GROUPS = 2
CHUNK = 64
ROW_PARTS = 4
ATTN_ROWS = 512
FOLD_PITCH = BAND_BLOCK + SUBLANES
POS_PER_ROW = LANES // (HEAD_DIM // 2)


def _dot(a, b):
    return jnp.dot(a, b, preferred_element_type=F32)


def _dot_nt(a, b):
    return lax.dot_general(a, b, (((1,), (1,)), ((), ())), preferred_element_type=F32)


def _dot_tn(a, b):
    return lax.dot_general(a, b, (((0,), (0,)), ((), ())), preferred_element_type=F32)


def _rms(x, gain):
    return x * lax.rsqrt(jnp.mean(x * x, axis=-1, keepdims=True) + RMS_EPS) * gain


def _sigmoid(x):
    return 1.0 / (1.0 + jnp.exp(-x))


def _segsum(x, seg):
    w = seg.shape[0]
    parts = [_dot(x[:, lo:lo + w].astype(BF16), seg) for lo in range(0, x.shape[1], w)]
    return parts[0] if len(parts) == 1 else jnp.concatenate(parts, axis=1)


def _const_spec(arr):
    nd = arr.ndim
    return pl.BlockSpec(arr.shape, lambda *_: (0,) * nd, pipeline_mode=pl.Buffered(1))


def _params(*sem):
    return pltpu.CompilerParams(dimension_semantics=sem, vmem_limit_bytes=VMEM_LIMIT_BYTES)


def _row_parts(n):
    return [slice(i * n // ROW_PARTS, (i + 1) * n // ROW_PARTS) for i in range(ROW_PARTS)]


def _cast_plan(weights, nsteps):
    specs, shapes = [], []
    for w in weights:
        rows, cols = w.shape
        every = 1
        while (rows * every) % nsteps or (rows * every // nsteps) % BF16_ROWS:
            every *= 2
        specs.append(pl.BlockSpec((rows * every // nsteps, cols), lambda i, every=every: (i // every, 0)))
        shapes.append(jax.ShapeDtypeStruct(w.shape, BF16))
    return specs, shapes


def _cast_slabs(src_refs, dst_refs):
    for src, dst in zip(src_refs, dst_refs):
        dst[...] = src[...].astype(BF16)


def _ffn1_kernel(ncast, x_ref, n1_ref, wg_ref, wu_ref, wd_ref, n2_ref, *rest):
    cast_in, (x1_ref, h2_ref), cast_out = rest[:ncast], rest[ncast:ncast + 2], rest[ncast + 2:]
    _cast_slabs(cast_in, cast_out)
    for rows in _row_parts(x_ref.shape[0]):
        x = x_ref[rows, :]
        h = _rms(x, n1_ref[...]).astype(BF16)
        g = _dot(h, wg_ref[...])
        u = _dot(h, wu_ref[...])
        act = (g * _sigmoid(g) * u).astype(BF16)
        x1 = x + 0.5 * _dot(act, wd_ref[...])
        x1_ref[rows, :] = x1
        h2_ref[rows, :] = _rms(x1, n2_ref[...]).astype(BF16)


def _ffn1(x2d, n1, wg, wu, wd, n2, next_weights):
    T = x2d.shape[0]
    nsteps = T // FFN_TM
    row = lambda i: (i, 0)
    cast_specs, cast_shapes = _cast_plan(next_weights, nsteps)
    return pl.pallas_call(
        functools.partial(_ffn1_kernel, len(next_weights)),
        grid=(nsteps,),
        in_specs=[pl.BlockSpec((FFN_TM, D_MODEL), row), _const_spec(n1), _const_spec(wg),
                  _const_spec(wu), _const_spec(wd), _const_spec(n2)] + cast_specs,
        out_specs=[pl.BlockSpec((FFN_TM, D_MODEL), row), pl.BlockSpec((FFN_TM, D_MODEL), row)] + cast_specs,
        out_shape=[jax.ShapeDtypeStruct((T, D_MODEL), F32), jax.ShapeDtypeStruct((T, D_MODEL), BF16)]
                  + cast_shapes,
        compiler_params=_params("arbitrary"),
        name="ffn1",
    )(x2d, n1, wg, wu, wd, n2, *next_weights)


def _inproj_kernel(ncast, h_ref, w_ref, *rest):
    cast_in, (zr_ref, za_ref, gt_ref), cast_out = rest[:ncast], rest[ncast:ncast + 3], rest[ncast + 3:]
    _cast_slabs(cast_in, cast_out)
    for rows in _row_parts(h_ref.shape[0]):
        h = h_ref[rows, :]
        zr_ref[rows, :] = _dot(h, w_ref[:, :RWKV_COLS])
        za_ref[rows, :] = _dot(h, w_ref[:, RWKV_COLS:RWKV_COLS + ATTN_COLS]).astype(BF16)
        gt_ref[rows, :] = _sigmoid(_dot(h, w_ref[:, RWKV_COLS + ATTN_COLS:])).astype(BF16)


def _inproj(h2, w, next_weights):
    T = h2.shape[0]
    nsteps = T // PROJ_TM
    row = lambda i: (i, 0)
    cast_specs, cast_shapes = _cast_plan(next_weights, nsteps)
    return pl.pallas_call(
        functools.partial(_inproj_kernel, len(next_weights)),
        grid=(nsteps,),
        in_specs=[pl.BlockSpec((PROJ_TM, D_MODEL), row), _const_spec(w)] + cast_specs,
        out_specs=[pl.BlockSpec((PROJ_TM, RWKV_COLS), row), pl.BlockSpec((PROJ_TM, ATTN_COLS), row),
                   pl.BlockSpec((PROJ_TM, GATE_COLS), row)] + cast_specs,
        out_shape=[jax.ShapeDtypeStruct((T, RWKV_COLS), F32), jax.ShapeDtypeStruct((T, ATTN_COLS), BF16),
                   jax.ShapeDtypeStruct((T, GATE_COLS), BF16)] + cast_shapes,
        compiler_params=_params("arbitrary"),
        name="inproj",
    )(h2, w, *next_weights)


def _scan_chunks(row_list, h, r_s, k_s, v_s, na_s, b_s, lw_s, cum_s, y_s, fillers=()):
    fillers = list(fillers)

    def fill():
        if fillers:
            fillers.pop(0)()

    C = CHUNK
    C2 = 2 * C
    pairs = range(RWKV_HEADS // 2)
    rows_of = [rows for rows in row_list for _ in pairs]
    lanes = [slice(pr * LANES, (pr + 1) * LANES) for _ in row_list for pr in pairs]
    ci = lax.broadcasted_iota(jnp.int32, (C, LANES), 1)
    ri = lax.broadcasted_iota(jnp.int32, (C, LANES), 0)
    first = ci < HEAD_DIM
    strict = ci % C < ri
    incl = ci % C <= ri
    eye = jnp.where(ci % C == ri, 1.0, 0.0)
    ri2 = lax.broadcasted_iota(jnp.int32, (C2, 2 * C2), 0)
    ci2 = lax.broadcasted_iota(jnp.int32, (C2, 2 * C2), 1)
    same_head = ri2 // C == (ci2 % C2) // C
    diag = lax.broadcasted_iota(jnp.int32, (C2, C2), 0) == lax.broadcasted_iota(jnp.int32, (C2, C2), 1)

    def stack(x):
        return jnp.concatenate([jnp.where(first, x, 0.0), jnp.where(first, 0.0, x)], axis=0)

    def cat(parts, axis):
        return jnp.concatenate(parts, axis=axis)

    def bf(x):
        return x.astype(BF16)

    def swap(x):
        return pltpu.roll(x, HEAD_DIM, 1)

    def load(ref):
        return [ref[rows, ln] for rows, ln in zip(rows_of, lanes)]

    lw = load(lw_s)
    cum = load(cum_s)
    p_inc = [jnp.exp(x) for x in cum]
    p_exc = [jnp.exp(x - y) for x, y in zip(cum, lw)]
    p_inv = [jnp.exp(-x) for x in cum]
    p_end = [x[C - 1:C, :] for x in p_inc]
    to_end = [x * y for x, y in zip(p_inv, p_end)]

    a_st = [stack(x * p) for x, p in zip(load(na_s), p_exc)]
    r_pk = [x * p for x, p in zip(load(r_s), p_inc)]
    v_pk = load(v_s)
    v_st = [bf(stack(x)) for x in v_pk]
    b = load(b_s)
    k = load(k_s)
    xr = [bf(cat([a, stack(r)], 0)) for a, r in zip(a_st, r_pk)]
    y_g = [bf(cat([bb * p, kk * p], 0)) for bb, kk, p in zip(b, k, p_inv)]
    y_h = [bf(cat([bb * p, kk * p], 0)) for bb, kk, p in zip(b, k, to_end)]

    g = [_dot_nt(x, y) for x, y in zip(xr, y_g)]
    l_ab = [jnp.where(strict & first, x[:C], jnp.where(strict, swap(x[C:C2]), 0.0)) for x in g]
    l_ak = [jnp.where(strict & first, swap(x[:C]), jnp.where(strict, x[C:C2], 0.0)) for x in g]
    m_rb = [jnp.where(incl & first, x[C2:C2 + C], jnp.where(incl, swap(x[C2 + C:]), 0.0)) for x in g]
    m_rk = [jnp.where(incl & first, swap(x[C2:C2 + C]), jnp.where(incl, x[C2 + C:], 0.0)) for x in g]

    pw = [_dot(bf(x), bf(stack(x))) for x in l_ab]
    fill()
    acc = [eye + x for x in l_ab]
    for _ in range(1, int(math.log2(C)) - 1):
        tp = [_dot(bf(cat([t, p], 0)), bf(stack(p))) for t, p in zip(acc, pw)]
        acc = [t + x[:C] for t, x in zip(acc, tp)]
        pw = [x[C:] for x in tp]
        fill()
    acc = [t + _dot(bf(t), bf(stack(p))) for t, p in zip(acc, pw)]

    lm = [_dot(bf(cat([x, y], 0)), vv) for x, y, vv in zip(l_ak, m_rk, v_st)]
    av = [_dot(bf(t), bf(cat([a, stack(x[:C])], 1))) for t, a, x in zip(acc, a_st, lm)]
    ry = [_dot(bf(m), bf(cat([stack(x[:, :LANES]), stack(x[:, LANES:])], 1))) + cat([r, y[C:]], 1)
          for m, x, r, y in zip(m_rb, av, r_pk, lm)]
    zero = jnp.zeros((C, LANES), F32)
    md = [jnp.where(same_head, _dot_tn(yh, bf(cat([x, cat([zero, vv], 1)], 0))), 0.0)
          for yh, x, vv in zip(y_h, av, v_pk)]
    r_new = [bf(y[:, :LANES]) for y in ry]
    m_new = [bf(jnp.where(diag, jnp.broadcast_to(pe, (C2, C2)), 0.0) + x[:, :C2])
             for pe, x in zip(p_end, md)]

    while fillers:
        fill()

    def step(ci_, rows):
        base = ci_ * len(pairs)
        hb = [bf(x) for x in h]
        yy = [_dot(r_new[base + pr], hb[pr]) for pr in pairs]
        for pr in pairs:
            h[pr] = _dot(m_new[base + pr], hb[pr]) + md[base + pr][:, C2:]
            y_s[rows, lanes[pr]] = yy[pr] + ry[base + pr][:, LANES:]

    return [functools.partial(step, ci_, rows) for ci_, rows in enumerate(row_list)]


def _rwkv_kernel(z_ref, zh_ref, mu_ref, w0_ref, a0_ref, w2a_ref, g2_ref, kk_ref, ka_ref, rk_ref,
                 gnw_ref, gnb_ref, seg_ref, tril_ref, y_ref,
                 h_ref, r_s, k_s, v_s, na_s, b_s, lw_s, cum_s, g_s, y_s):
    t = pl.program_id(1)
    TS = RWKV_TS
    D = RWKV_DIM

    @pl.when(t == 0)
    def _():
        h_ref[...] = jnp.zeros_like(h_ref)

    has_prev = jnp.where(t > 0, 1.0, 0.0).astype(F32)

    def shifted(lo, hi):
        zc = z_ref[:, lo:hi]
        prev = zh_ref[SUBLANES - 1:SUBLANES, lo:hi] * has_prev
        rolled = pltpu.roll(zc, 1, 0)
        row = lax.broadcasted_iota(jnp.int32, (SUBLANES, hi - lo), 0)
        zp = jnp.concatenate([jnp.where(row == 0, prev, rolled[:SUBLANES]), rolled[SUBLANES:]], axis=0)
        return zc + (zp - zc) * mu_ref[:, lo:hi]

    seg = seg_ref[...]
    r = shifted(0, D)
    k = shifted(D, 2 * D)
    v = shifted(2 * D, 3 * D)
    wa = shifted(3 * D, 3 * D + DECAY_LORA + ICLR_LORA)
    gd = shifted(3 * D + DECAY_LORA + ICLR_LORA, RWKV_COLS)

    lane = lax.broadcasted_iota(jnp.int32, wa.shape, 1)
    lora_in = jnp.where(lane < DECAY_LORA, jnp.tanh(wa), wa).astype(BF16)
    lora = _dot(lora_in, w2a_ref[...])
    lw = -math.exp(-0.5) * _sigmoid(w0_ref[...] + lora[:, :D])
    lw_s[...] = lw
    tril = tril_ref[...]
    hi = lw.astype(BF16)
    lo = (lw - hi.astype(F32)).astype(BF16)
    span = tril.shape[0]
    for r0 in range(0, TS, span):
        cum_s[r0:r0 + span, :] = _dot(tril, hi[r0:r0 + span]) + _dot(tril, lo[r0:r0 + span])
    a = _sigmoid(a0_ref[...] + lora[:, D:])
    g_s[...] = _dot(_sigmoid(gd).astype(BF16), g2_ref[...])

    kk = k * kk_ref[...]
    kk = kk * lax.rsqrt(jnp.maximum(_segsum(kk * kk, seg), 1e-24))
    k2 = k * (1.0 + (a - 1.0) * ka_ref[...])
    r_s[...] = r
    k_s[...] = k2
    v_s[...] = v
    na_s[...] = -kk
    b_s[...] = kk * a

    row_list = [slice(c * CHUNK, (c + 1) * CHUNK) for c in range(TS // CHUNK)]
    per_group = len(row_list) // SCAN_GROUPS
    h = [h_ref[pr] for pr in range(RWKV_HEADS // 2)]
    steps = []
    for grp in range(SCAN_GROUPS):
        steps = _scan_chunks(row_list[grp * per_group:(grp + 1) * per_group], h,
                             r_s, k_s, v_s, na_s, b_s, lw_s, cum_s, y_s, fillers=steps)
    for step in steps:
        step()
    for pr, state in enumerate(h):
        h_ref[pr] = state

    y = y_s[...]
    inv_n = 1.0 / HEAD_DIM
    mean = _segsum(y, seg) * inv_n
    yc = y - mean
    var = _segsum(yc * yc, seg) * inv_n
    yn = yc * lax.rsqrt(var + GN_EPS) * gnw_ref[...] + gnb_ref[...]
    bonus = _segsum(r_s[...] * k_s[...] * rk_ref[...], seg) * v_s[...]
    y_ref[...] = ((yn + bonus) * g_s[...]).astype(BF16)


def _rwkv(zr, mu, w0, a0, w2a, g2, k_k, k_a, r_k, gn_w, gn_b, seg, B, S):
    TS = RWKV_TS
    t_idx = np.arange(2 * CHUNK)
    tril = jnp.asarray((t_idx[:, None] // CHUNK == t_idx[None, :] // CHUNK)
                       & (t_idx[None, :] <= t_idx[:, None]), BF16)
    consts = [mu, w0, a0, w2a, g2, k_k, k_a, r_k, gn_w, gn_b, seg, tril]
    halo_blocks = TS // SUBLANES
    stage = pltpu.VMEM((TS, RWKV_DIM), F32)
    return pl.pallas_call(
        _rwkv_kernel,
        grid=(B, S // TS),
        in_specs=[pl.BlockSpec((None, TS, RWKV_COLS), lambda b, t: (b, t, 0)),
                  pl.BlockSpec((None, SUBLANES, RWKV_COLS),
                               lambda b, t: (b, jnp.maximum(t * halo_blocks - 1, 0), 0))]
                 + [_const_spec(c) for c in consts],
        out_specs=pl.BlockSpec((None, TS, RWKV_DIM), lambda b, t: (b, t, 0)),
        out_shape=jax.ShapeDtypeStruct((B, S, RWKV_DIM), BF16),
        scratch_shapes=[pltpu.VMEM((RWKV_HEADS // 2, 2 * CHUNK, 2 * CHUNK), F32)] + [stage] * 9,
        compiler_params=_params("parallel", "arbitrary"),
        name="rwkv",
    )(zr, zr, *consts)


Q_SLAB, K_SLAB, V_SLAB, N_SLABS = 0, 4, 6, 10


def _fold_rows(tile, sub, dil):
    return pl.ds(tile * (ATTN_ROWS // dil) + sub, dil, stride=FOLD_PITCH)


def _attn_group(gi, dil, S, x_s, o_s, l_s, eye_ref, bias_ref):
    L = S // dil
    nb = L // BAND_BLOCK
    Q = BAND_BLOCK
    heads = range(HEADS_PER_GROUP)
    lane2 = lax.broadcasted_iota(jnp.int32, (Q, 2 * LANES), 1)
    hmask = [lane2 // HEAD_DIM == j for j in heads]
    eye = eye_ref[...]

    def load(slab, rows):
        return x_s[slab, rows, :].astype(BF16)

    units = [divmod(u, nb) for u in range(dil * nb)]
    cur, q, k_aug, v = [], [], [], []
    for res, blk in units:
        if nb > 1:
            rows = pl.ds(res + blk * (Q * dil), Q, stride=dil)
        else:
            rows = pl.ds(res * FOLD_PITCH, Q)
        if blk > 0:
            prev = pl.ds(res + (blk - 1) * (Q * dil), Q, stride=dil)
            both = lambda slab, prev=prev, rows=rows: jnp.concatenate([load(slab, prev), load(slab, rows)], axis=0)
            bias = bias_ref[...]
        else:
            both = lambda slab, rows=rows: load(slab, rows)
            bias = bias_ref[Q:, :]
        cur.append(rows)
        q.append([jnp.concatenate([load(Q_SLAB + j, rows), eye], axis=1) for j in heads])
        k_aug.append([jnp.concatenate([both(K_SLAB + hp), bias], axis=1) for hp in range(2)])
        v.append([both(V_SLAB + j) for j in heads])
    chains = [(n, j) for n in range(len(units)) for j in heads]
    s = [_dot_nt(q[n][j], k_aug[n][j // 2]) for n, j in chains]
    m = [jnp.max(x, axis=-1, keepdims=True) for x in s]
    e = [jnp.exp2(x - y) for x, y in zip(s, m)]
    l = [jnp.sum(x, axis=-1, keepdims=True) for x in e]
    p = [x.astype(BF16) for x in e]
    inv = [1.0 / y for y in l]
    first = lax.broadcasted_iota(jnp.int32, (Q, LANES), 1) < HEAD_DIM
    for n in range(len(units)):
        c = n * HEADS_PER_GROUP
        lse = jnp.broadcast_to(m[c] * LN2 + jnp.log(l[c]), (Q, 2 * LANES))
        for j in heads[1:]:
            lse = jnp.where(hmask[j], m[c + j] * LN2 + jnp.log(l[c + j]), lse)
        for hp in range(2):
            pp = jnp.concatenate([p[c + 2 * hp], p[c + 2 * hp + 1]], axis=1)
            vv = jnp.concatenate([v[n][2 * hp], v[n][2 * hp + 1]], axis=0)
            o_s[gi, hp, cur[n], :] = _dot(pp, vv) * jnp.where(first, inv[c + 2 * hp], inv[c + 2 * hp + 1])
            l_s[gi, hp, cur[n], :] = lse[:, hp * LANES:(hp + 1) * LANES]


def _attn_kernel(q_ref, k_ref, v_ref, pos_ref, invf_ref, qg_ref, qgp_ref, kg_ref, kgp_ref, seg_ref, rot_ref,
                 eye_ref, bias_ref, spread_ref, out_ref, x_s, o_s, l_s, cs_s):
    g = pl.program_id(1)
    S = q_ref.shape[0]
    R = ATTN_ROWS
    ng = len(ATTN_GROUPS)
    fold_dil = ATTN_GROUPS[-1][1]

    @pl.when(g == 0)
    def _():
        rc = R

        def body(i, carry):
            ang = pos_ref[pl.ds(pl.multiple_of(i * rc, rc), rc), :].astype(F32) * invf_ref[...]
            for tbl, fn in enumerate((jnp.cos, jnp.sin)):
                val = fn(ang)
                hi = val.astype(BF16)
                rest = val - hi.astype(F32)
                mid = rest.astype(BF16)
                lo = (rest - mid.astype(F32)).astype(BF16)
                q_gain, k_gain = ((qg_ref, kg_ref), (qgp_ref, kgp_ref))[tbl]
                for sub in range(POS_PER_ROW):
                    spread = spread_ref[sub]
                    full = _dot(hi, spread) + _dot(mid, spread) + _dot(lo, spread)
                    rows = pl.ds(i * (rc * POS_PER_ROW) + sub, rc, stride=POS_PER_ROW)
                    cs_s[tbl, rows, :] = full * (q_gain[...] * (HEAD_DIM ** -0.5 * LOG2E))
                    cs_s[2 + tbl, rows, :] = full * k_gain[...]
            return carry
        lax.fori_loop(0, S // (rc * POS_PER_ROW), body, 0)

    seg = seg_ref[...]
    rot = rot_ref[...]
    first = lax.broadcasted_iota(jnp.int32, (R, LANES), 1) < HEAD_DIM

    def prep(folded, i, carry):
        rows = pl.ds(pl.multiple_of(i * R, R), R)

        def store(slab, val):
            if folded:
                for sub in range(R // fold_dil):
                    x_s[slab, _fold_rows(i, sub, fold_dil), :] = val[sub * fold_dil:(sub + 1) * fold_dil]
            else:
                x_s[slab, rows, :] = val

        def store_per_head(slab, val):
            store(slab, jnp.where(first, val, 0.0))
            store(slab + 1, jnp.where(first, 0.0, val))

        def normed(src, hf, tables):
            x = src[rows, hf * LANES:(hf + 1) * LANES]
            xf = x.astype(F32)
            inv = lax.rsqrt(_segsum(xf * xf, seg) + RMS_EPS)
            xr = _dot(x, rot)
            return (xf * cs_s[tables, rows, :] + xr * cs_s[tables + 1, rows, :]) * inv

        for hf in range(2):
            store_per_head(Q_SLAB + 2 * hf, normed(q_ref, hf, 0))
            store(K_SLAB + hf, normed(k_ref, hf, 2))
            store_per_head(V_SLAB + 2 * hf, v_ref[rows, hf * LANES:(hf + 1) * LANES].astype(F32))
        return carry

    @pl.when(g < ng - 1)
    def _():
        lax.fori_loop(0, S // R, functools.partial(prep, False), 0)

    @pl.when(g == ng - 1)
    def _():
        lax.fori_loop(0, S // R, functools.partial(prep, True), 0)

    for gi, (_, dil) in enumerate(ATTN_GROUPS):
        pl.when(g == gi)(functools.partial(_attn_group, gi, dil, S, x_s, o_s, l_s, eye_ref, bias_ref))

    @pl.when(g == ng - 1)
    def _():
        def body(i, carry):
            rows = pl.ds(pl.multiple_of(i * R, R), R)

            def fetch(ref, gi, slab):
                if gi < ng - 1:
                    return ref[gi, slab, rows, :]
                return jnp.concatenate([ref[gi, slab, _fold_rows(i, sub, fold_dil), :]
                                        for sub in range(R // fold_dil)], axis=0)

            for slab in range(2):
                ls = [fetch(l_s, gi, slab) for gi in range(ng)]
                m = jnp.maximum(jnp.maximum(ls[0], ls[1]), ls[2])
                ws = [jnp.exp(x - m) for x in ls]
                num = sum(w * fetch(o_s, gi, slab) for gi, w in enumerate(ws))
                out_ref[rows, slab * LANES:(slab + 1) * LANES] = (num / (ws[0] + ws[1] + ws[2])).astype(BF16)
            return carry
        lax.fori_loop(0, S // R, body, 0)


def _attn(za, pos, consts, B, S):
    ng = len(ATTN_GROUPS)
    gw = ATTN_OUT_DIM
    rows = (S // BAND_BLOCK) * FOLD_PITCH
    return pl.pallas_call(
        _attn_kernel,
        grid=(B, ng),
        in_specs=[pl.BlockSpec((None, S, gw), lambda b, g: (b, 0, g)),
                  pl.BlockSpec((None, S, gw), lambda b, g: (b, 0, ng + g)),
                  pl.BlockSpec((None, S, gw), lambda b, g: (b, 0, 2 * ng + g)),
                  pl.BlockSpec((None, S // POS_PER_ROW, LANES), lambda b, g: (b, 0, 0))]
                 + [_const_spec(c) for c in consts],
        out_specs=pl.BlockSpec((None, S, ATTN_OUT_DIM), lambda b, g: (b, 0, 0)),
        out_shape=jax.ShapeDtypeStruct((B, S, ATTN_OUT_DIM), BF16),
        scratch_shapes=[pltpu.VMEM((N_SLABS, rows, LANES), F32), pltpu.VMEM((ng, 2, rows, LANES), F32),
                        pltpu.VMEM((ng, 2, rows, LANES), F32), pltpu.VMEM((4, S, LANES), F32)],
        compiler_params=_params("parallel", "arbitrary"),
        name="attn",
    )(za, za, za, pos, *consts)


def _out_kernel(x1_ref, gt_ref, yr_ref, ya_ref, p_ref, wbr_ref, wba_ref, wo_ref, n2_ref, wg_ref, wu_ref,
                wd_ref, pn_ref, pwg_ref, pwp_ref, out_ref):
    br = _dot(yr_ref[...], wbr_ref[...])
    ba = _dot(ya_ref[...], wba_ref[...])
    merged = gt_ref[:, :D_MODEL].astype(F32) * br + gt_ref[:, D_MODEL:].astype(F32) * ba
    x2 = x1_ref[...] + _dot(merged.astype(BF16), wo_ref[...])
    h = _rms(x2, n2_ref[...]).astype(BF16)
    g = _dot(h, wg_ref[...])
    u = _dot(h, wu_ref[...])
    act = (g * _sigmoid(g) * u).astype(BF16)
    x3 = x2 + 0.5 * _dot(act, wd_ref[...])
    hp = _rms(x3, pn_ref[...]).astype(BF16)
    gate = _sigmoid(_dot(hp, pwg_ref[...]))
    out_ref[...] = x3 + gate * _dot(p_ref[...].astype(BF16), pwp_ref[...])


def _out(x1, gt, yr, ya, p2d, consts):
    T = x1.shape[0]
    TM = OUT_TM
    row = lambda i: (i, 0)
    return pl.pallas_call(
        _out_kernel,
        grid=(T // TM,),
        in_specs=[pl.BlockSpec((TM, D_MODEL), row), pl.BlockSpec((TM, GATE_COLS), row),
                  pl.BlockSpec((TM, RWKV_DIM), row), pl.BlockSpec((TM, ATTN_OUT_DIM), row),
                  pl.BlockSpec((TM, PLE_DIM), row)] + [_const_spec(c) for c in consts],
        out_specs=pl.BlockSpec((TM, D_MODEL), row),
        out_shape=jax.ShapeDtypeStruct((T, D_MODEL), F32),
        compiler_params=_params("parallel"),
        name="merge_ffn2",
    )(x1, gt, yr, ya, p2d, *consts)


def _rotate_half_matrix():
    half = HEAD_DIM // 2
    m = np.zeros((LANES, LANES), np.float32)
    for lane in range(LANES):
        if lane % HEAD_DIM < half:
            m[lane + half, lane] = -1.0
        else:
            m[lane - half, lane] = 1.0
    return jnp.asarray(m, BF16)


def _band_bias():
    kq = np.arange(BAND_BLOCK)[:, None]
    qq = np.arange(BAND_BLOCK)[None, :]
    cur = np.where(kq <= qq, 0.0, NEG_INF)
    prev = np.where(kq >= qq, 0.0, NEG_INF)
    return jnp.asarray(np.concatenate([prev, cur]), BF16)


def _block_ones(n, blk):
    idx = np.arange(n) // blk
    return jnp.asarray(idx[:, None] == idx[None, :], BF16)


def kernel(x, p, positions, ffn1_norm, ffn1_w_gate, ffn1_w_up, ffn1_w_down, mix_norm, w_in, rwkv_mu, rwkv_w0, rwkv_w2, rwkv_a0, rwkv_a2, rwkv_g2, rwkv_k_k, rwkv_k_a, rwkv_r_k, rwkv_gn_w, rwkv_gn_b, q_norm, k_norm, w_br_rwkv, w_br_attn, w_out, ffn2_norm, ffn2_w_gate, ffn2_w_up, ffn2_w_down, ple_norm, ple_w_gate, ple_w_proj):
    B, S, D = x.shape
    depth = p.shape[0]
    T = B * S
    half = HEAD_DIM // 2
    bf = lambda w: w.astype(BF16)
    rowvec = lambda a: a.reshape(1, -1).astype(F32)

    inv_freq = 1.0 / (ROPE_THETA ** (jnp.arange(0, HEAD_DIM, 2, dtype=F32) / HEAD_DIM))
    invf = jnp.tile(inv_freq, HEADS_PER_GROUP).reshape(1, LANES)
    pos = jnp.repeat(positions.reshape(B, S // POS_PER_ROW, POS_PER_ROW), half, axis=2)
    lane = np.arange(LANES)
    spread = jnp.asarray(np.stack([lane[:, None] == sub * half + lane[None, :] % half
                                   for sub in range(POS_PER_ROW)]), BF16)
    seg64 = _block_ones(2 * LANES, HEAD_DIM)
    seg_mean = _block_ones(LANES, HEAD_DIM) * (1.0 / HEAD_DIM)
    rot = _rotate_half_matrix()
    eye = jnp.eye(BAND_BLOCK, dtype=BF16)
    heads_per_slab = LANES // HEAD_DIM
    gain = lambda a: jnp.tile(a, heads_per_slab).reshape(1, LANES).astype(F32)
    partner = lambda a: gain(jnp.concatenate([a[half:], a[:half]]))

    xc = x.reshape(T, D)
    for i in range(depth):
        zeros = jnp.zeros((DECAY_LORA, RWKV_DIM), F32)
        w2a = jnp.concatenate([jnp.concatenate([rwkv_w2[i], zeros], axis=1),
                               jnp.concatenate([zeros, rwkv_a2[i]], axis=1)], axis=0)

        x1, h2, w_in_bf = _ffn1(xc, rowvec(ffn1_norm[i]), bf(ffn1_w_gate[i]), bf(ffn1_w_up[i]), bf(ffn1_w_down[i]),
                                rowvec(mix_norm[i]), [w_in[i]])
        out_weights = [w_br_rwkv[i], w_br_attn[i], w_out[i], ffn2_w_gate[i], ffn2_w_up[i], ffn2_w_down[i],
                       ple_w_gate[i], ple_w_proj[i]]
        zr, za, gt, wbr, wba, wo, wg2, wu2, wd2, pwg, pwp = _inproj(h2, w_in_bf, out_weights)
        yr = _rwkv(zr.reshape(B, S, RWKV_COLS), rowvec(rwkv_mu[i]), rowvec(rwkv_w0[i]), rowvec(rwkv_a0[i]),
                   bf(w2a), bf(rwkv_g2[i]), rowvec(rwkv_k_k[i]), rowvec(rwkv_k_a[i]), rowvec(rwkv_r_k[i]),
                   rowvec(rwkv_gn_w[i]), rowvec(rwkv_gn_b[i]), seg64, B, S)
        attn_consts = [invf, gain(q_norm[i]), partner(q_norm[i]), gain(k_norm[i]), partner(k_norm[i]),
                       seg_mean, rot, eye, _band_bias(), spread]
        ya = _attn(za.reshape(B, S, ATTN_COLS), pos, attn_consts, B, S)
        consts = [wbr, wba, wo, rowvec(ffn2_norm[i]), wg2, wu2, wd2, rowvec(ple_norm[i]), pwg, pwp]
        xc = _out(x1, gt, yr.reshape(T, RWKV_DIM), ya.reshape(T, ATTN_OUT_DIM), p[i].reshape(T, PLE_DIM), consts)
    return xc.reshape(B, S, D)
```
